```python
import jax
import jax.numpy as jnp
from jax import lax
import numpy as np

D_MODEL = 1024
BATCH = 8
SEQ = 2048
DEPTH = 4
DEC_BATCH = 128
DEC_SEQ = 8
PAST_LEN = 2048
PAGE_SIZE = 128

F32 = jnp.float32
N_MIXERS = 3
N_LAYERS_A = (DEPTH + 2) // 3
N_LAYERS_B = (DEPTH + 1) // 3
N_LAYERS_C = DEPTH // 3

DEEPNORM_ALPHA = (2.0 * DEPTH) ** 0.25
DEEPNORM_BETA = (8.0 * DEPTH) ** -0.25
LN_EPS = 1e-5

A_HEAD_DIM = 64
A_HEADS = D_MODEL // A_HEAD_DIM
A_WIDTH = A_HEADS * A_HEAD_DIM
A_LORA_W = 64
A_LORA_A = 64
A_NCOLS = 4 * A_WIDTH + A_LORA_W + A_LORA_A
A_GN_EPS = 64e-5
_A_SPLITS = (A_WIDTH, 2 * A_WIDTH, 3 * A_WIDTH, 4 * A_WIDTH, 4 * A_WIDTH + A_LORA_W)

B_WIDTH = D_MODEL
B_GROUPS = 4
B_GROUP_W = B_WIDTH // B_GROUPS
B_WINDOWS = (2, 4, 8, 16)
B_BUF = max(B_WINDOWS) - 1

C_HEAD_DIM = 64
C_HEADS = 16
C_KV_HEADS = 4
C_HPG = C_HEADS // C_KV_HEADS
C_WIDTH = C_HEADS * C_HEAD_DIM
C_KV_WIDTH = C_KV_HEADS * C_HEAD_DIM
C_CMP_BLOCK = 32
C_SEL_BLOCK = 64
C_TOP_N = 16
C_WINDOW = 512
C_Q_ROWS = 256
C_FORCE = 1e9
C_NCOLS = 2 * C_WIDTH + 6 * C_KV_WIDTH + 3 * C_HEADS

kernel_name = 'hybrid_rwkv7_pool_nsa_deepnorm_step'


def _layer_norm(x, g, b):
    xf = x.astype(F32)
    mu = jnp.mean(xf, axis=-1, keepdims=True)
    var = jnp.mean(jnp.square(xf - mu), axis=-1, keepdims=True)
    return ((xf - mu) * lax.rsqrt(var + LN_EPS) * g.astype(F32) + b.astype(F32)).astype(x.dtype)


def _masked_softmax(s, mask, axis=-1):
    s = jnp.where(mask, s.astype(F32), -jnp.inf)
    m = jnp.max(s, axis=axis, keepdims=True)
    m = jnp.where(jnp.isfinite(m), m, 0.0)
    e = jnp.where(mask, jnp.exp(s - m), 0.0)
    return e / jnp.maximum(jnp.sum(e, axis=axis, keepdims=True), 1e-30)


def _alibi_slopes(n):
    return jnp.power(2.0, -8.0 * (jnp.arange(n, dtype=F32) + 1.0) / n)


def _query_block(batch, t):
    qb = max(1, min(t, C_Q_ROWS // batch))
    while t % qb:
        qb -= 1
    return qb


def _gather_pages(pool, page_table):
    pages = pool[page_table]
    return pages.reshape(page_table.shape[0], page_table.shape[1] * pool.shape[1], pool.shape[2], pool.shape[3])


def _rwkv_mixer(x, S0, p_prev, w_in, mu, w0, w2, a0, a2, k_k, k_a, r_k, lnx_g, lnx_b, w_out):
    B, T, _ = x.shape
    H, N = A_HEADS, A_HEAD_DIM
    p = x @ w_in
    p_shift = jnp.concatenate([p_prev[:, None].astype(p.dtype), p[:, :-1]], axis=1)
    pm = p + (p_shift - p) * mu
    r, k, v, z, lw, la = jnp.split(pm, list(_A_SPLITS), axis=-1)
    heads = lambda t: t.astype(F32).reshape(B, T, H, N)
    w_log = -jax.nn.softplus(-(w0 + jnp.tanh(lw) @ w2).astype(F32)) - 0.5
    decay = heads(jnp.exp(-jnp.exp(w_log)))
    a = heads(jax.nn.sigmoid((a0 + la @ a2).astype(F32)))
    r, k, v = heads(r), heads(k), heads(v)
    kk = k * k_k.astype(F32).reshape(H, N)
    kk = kk * lax.rsqrt(jnp.maximum(jnp.sum(kk * kk, axis=-1, keepdims=True), 1e-24))
    k = k * (1.0 + (a - 1.0) * k_a.astype(F32).reshape(H, N))

    def step(S, inp):
        r_t, w_t, k_t, v_t, kk_t, a_t = inp
        sa = jnp.einsum('bhij,bhj->bhi', S, -kk_t)
        S = (S * w_t[:, :, None, :] + sa[..., None] * (kk_t * a_t)[:, :, None, :]
             + v_t[..., None] * k_t[:, :, None, :])
        return S, jnp.einsum('bhij,bhj->bhi', S, r_t)

    seq = tuple(jnp.swapaxes(t, 0, 1) for t in (r, decay, k, v, kk, a))
    S_final, y = lax.scan(step, S0.astype(F32), seq)
    y = jnp.swapaxes(y, 0, 1)
    mean = jnp.mean(y, axis=-1, keepdims=True)
    var = jnp.mean(jnp.square(y - mean), axis=-1, keepdims=True)
    y = (y - mean) * lax.rsqrt(var + A_GN_EPS)
    y = y * lnx_g.astype(F32).reshape(H, N) + lnx_b.astype(F32).reshape(H, N)
    y = y + jnp.sum(r * k * r_k.astype(F32), axis=-1, keepdims=True) * v
    y = y.reshape(B, T, A_WIDTH).astype(x.dtype)
    out = (y * jax.nn.silu(z)) @ w_out
    return out, S_final, p[:, -1]


def _pool_mixer(x, buf, front_valid, w_in, w_grp, scale, w_out):
    B, T, _ = x.shape
    u, z = jnp.split(x @ w_in, 2, axis=-1)
    ext = jnp.concatenate([buf.astype(u.dtype), u], axis=1)
    valid = jnp.concatenate([jnp.full((B_BUF,), front_valid, dtype=bool),
                             jnp.ones((T,), dtype=bool)]).astype(F32)
    csum = jnp.concatenate([jnp.zeros((B, 1, B_WIDTH), F32),
                            jnp.cumsum(ext.astype(F32) * valid[None, :, None], axis=1)], axis=1)
    ccnt = jnp.concatenate([jnp.zeros((1,), F32), jnp.cumsum(valid)])
    end = B_BUF + 1 + jnp.arange(T)
    pooled = []
    for gi, w in enumerate(B_WINDOWS):
        cs = csum[:, :, gi * B_GROUP_W:(gi + 1) * B_GROUP_W]
        cnt = ccnt[end] - ccnt[end - w]
        pooled.append((cs[:, end] - cs[:, end - w]) / cnt[None, :, None])
    d = jnp.stack(pooled, axis=2) - u.astype(F32).reshape(B, T, B_GROUPS, B_GROUP_W)
    y = jnp.einsum('btgi,gij->btgj', d, w_grp).reshape(B, T, B_WIDTH) * scale
    out = (y.astype(x.dtype) * jax.nn.silu(z)) @ w_out
    return out, ext[:, -B_BUF:]


def _nsa_attend(q, gates, kc, vc, ks, vs, kw_ext, vw_ext, q_off, n_front, cmp_wk, cmp_wv):
    B, T = q.shape[0], q.shape[1]
    Tk = kc.shape[1]
    G, HPG, DH, SB = C_KV_HEADS, C_HPG, C_HEAD_DIM, C_SEL_BLOCK
    scale = DH ** -0.5
    slopes = _alibi_slopes(C_HEADS).reshape(G, HPG)[None, :, :, None, None]
    n_cmp = Tk // C_CMP_BLOCK
    blk = lambda t: t[:, :n_cmp * C_CMP_BLOCK].reshape(B, n_cmp, C_CMP_BLOCK, G, DH)
    k_cmp = jnp.einsum('bnlgd,l->bngd', blk(kc), cmp_wk)
    v_cmp = jnp.einsum('bnlgd,l->bngd', blk(vc), cmp_wv)
    cmp_end = (jnp.arange(n_cmp) + 1) * C_CMP_BLOCK - 1
    n_sel = -(-Tk // SB)
    pad = n_sel * SB - Tk
    selb = lambda t: jnp.pad(t, ((0, 0), (0, pad), (0, 0), (0, 0))).reshape(
        B, n_sel, SB, G, DH).transpose(0, 3, 1, 2, 4)
    k_selb, v_selb = selb(ks), selb(vs)
    ratio = SB // C_CMP_BLOCK
    top_n = min(C_TOP_N, n_sel)
    qb_size = _query_block(B, T)
    nqb = T // qb_size
    q_blocks = q.reshape(B, nqb, qb_size, G, HPG, DH).transpose(1, 0, 3, 4, 2, 5)
    g_blocks = gates.reshape(B, nqb, qb_size, G, HPG, 3).transpose(1, 0, 3, 4, 2, 5)
    starts = jnp.arange(nqb) * qb_size
    bi = jnp.arange(B)[:, None, None, None]
    gi = jnp.arange(G)[None, :, None, None]
    n_win = n_front + qb_size
    blk_id = jnp.arange(n_sel)

    def one_block(inp):
        qb, gb, i0 = inp
        qpos = q_off + i0 + jnp.arange(qb_size)
        dist_c = qpos[:, None] - cmp_end[None, :]
        s_c = jnp.einsum('bgjqd,bngd->bgjqn', qb, k_cmp).astype(F32) * scale - slopes * dist_c
        p_c = _masked_softmax(s_c, dist_c >= 0)
        o_c = jnp.einsum('bgjqn,bngd->bgjqd', p_c, v_cmp)
        imp = jnp.pad(p_c.sum(axis=2), ((0, 0), (0, 0), (0, 0), (0, n_sel * ratio - n_cmp)))
        imp = imp.reshape(B, G, qb_size, n_sel, ratio).sum(-1)
        cur = qpos // SB
        imp = jnp.where(blk_id[None, :] == cur[:, None], C_FORCE, imp)
        imp = jnp.where(blk_id[None, :] <= cur[:, None], imp, -1.0)
        top_v, top_i = lax.top_k(imp, top_n)
        k_g = k_selb[bi, gi, top_i]
        v_g = v_selb[bi, gi, top_i]
        kpos = top_i[..., None] * SB + jnp.arange(SB)
        dist_s = (qpos[:, None, None] - kpos)[:, :, None]
        mask_s = (dist_s >= 0) & (top_v >= 0)[:, :, None, :, :, None]
        s_s = jnp.einsum('bgjqd,bgqnld->bgjqnl', qb, k_g).astype(F32) * scale - slopes[..., None] * dist_s
        p_s = _masked_softmax(s_s, mask_s, axis=(-2, -1))
        o_s = jnp.einsum('bgjqnl,bgqnld->bgjqd', p_s, v_g)
        k_w = lax.dynamic_slice_in_dim(kw_ext, i0, n_win, axis=1)
        v_w = lax.dynamic_slice_in_dim(vw_ext, i0, n_win, axis=1)
        kwpos = q_off - n_front + i0 + jnp.arange(n_win)
        dist_w = qpos[:, None] - kwpos[None, :]
        mask_w = (kwpos[None, :] >= 0) & (dist_w >= 0) & (dist_w < C_WINDOW)
        s_w = jnp.einsum('bgjqd,bkgd->bgjqk', qb, k_w).astype(F32) * scale - slopes * dist_w
        p_w = _masked_softmax(s_w, mask_w)
        o_w = jnp.einsum('bgjqk,bkgd->bgjqd', p_w, v_w)
        gb = gb.astype(F32)
        return gb[..., 0:1] * o_c + gb[..., 1:2] * o_s + gb[..., 2:3] * o_w

    o = lax.map(one_block, (q_blocks, g_blocks, starts))
    return o.transpose(1, 0, 4, 2, 3, 5).reshape(B, T, C_WIDTH)


def _nsa_mixer(x, past, win_k_buf, win_v_buf, q_off, win_keep, w_in, cmp_wk, cmp_wv, w_out):
    B, T, _ = x.shape
    widths = [C_WIDTH] + [C_KV_WIDTH] * 6 + [3 * C_HEADS, C_WIDTH]
    splits = [int(v) for v in np.cumsum(widths)[:-1]]
    q, kc, vc, ks, vs, kw, vw, g, z = jnp.split(x @ w_in, splits, axis=-1)
    kvh = lambda t: t.reshape(B, T, C_KV_HEADS, C_HEAD_DIM)
    kc, vc, ks, vs, kw, vw = kvh(kc), kvh(vc), kvh(ks), kvh(vs), kvh(kw), kvh(vw)
    new_rows = (kc, vc, ks, vs)
    if past is None:
        full = list(new_rows)
    else:
        full = [jnp.concatenate([pst.astype(r.dtype), r], axis=1) for pst, r in zip(past, new_rows)]
    kw_ext = jnp.concatenate([win_k_buf.astype(kw.dtype), kw], axis=1)
    vw_ext = jnp.concatenate([win_v_buf.astype(vw.dtype), vw], axis=1)
    gates = jax.nn.sigmoid(g.astype(F32)).reshape(B, T, C_HEADS, 3)
    o = _nsa_attend(q.reshape(B, T, C_HEADS, C_HEAD_DIM), gates, full[0], full[1], full[2], full[3],
                    kw_ext, vw_ext, q_off, win_k_buf.shape[1], cmp_wk, cmp_wv)
    y = (o.astype(x.dtype) * jax.nn.silu(z)) @ w_out
    return y, new_rows, kw_ext[:, -win_keep:], vw_ext[:, -win_keep:]


def setup_inputs(seed: int = 0) -> dict:
    key = jax.random.key(seed)
    keys = iter(jax.random.split(key, 64))
    nrm = lambda shape, s=1.0: s * jax.random.normal(next(keys), shape, F32)
    uni = lambda shape, lo, hi: jax.random.uniform(next(keys), shape, F32, lo, hi)
    n_pages = PAST_LEN // PAGE_SIZE
    n_pool = (5 * DEC_BATCH * n_pages) // 4
    wb = min(C_WINDOW, PAST_LEN)
    page_table = jax.random.permutation(next(keys), n_pool)[:DEC_BATCH * n_pages].reshape(
        DEC_BATCH, n_pages).astype(jnp.int32)
    kv_pool = (N_LAYERS_C, n_pool, PAGE_SIZE, C_KV_HEADS, C_HEAD_DIM)
    kv_win = (N_LAYERS_C, DEC_BATCH, wb, C_KV_HEADS, C_HEAD_DIM)
    return {
        'x_prompt': nrm((BATCH, SEQ, D_MODEL)),
        'x_sample': nrm((DEC_BATCH, DEC_SEQ, D_MODEL)),
        'state_rwkv_S': nrm((N_LAYERS_A, DEC_BATCH, A_HEADS, A_HEAD_DIM, A_HEAD_DIM), 0.3),
        'state_rwkv_shift': nrm((N_LAYERS_A, DEC_BATCH, A_NCOLS)),
        'state_pool': nrm((N_LAYERS_B, DEC_BATCH, B_BUF, B_WIDTH)),
        'cache_cmp_k': nrm(kv_pool),
        'cache_cmp_v': nrm(kv_pool),
        'cache_sel_k': nrm(kv_pool),
        'cache_sel_v': nrm(kv_pool),
        'state_win_k': nrm(kv_win),
        'state_win_v': nrm(kv_win),
        'page_table': page_table,
        'ln_g': 1.0 + nrm((DEPTH, D_MODEL), 0.05),
        'ln_b': nrm((DEPTH, D_MODEL), 0.02),
        'a_w_in': nrm((N_LAYERS_A, D_MODEL, A_NCOLS), D_MODEL ** -0.5),
        'a_mu': uni((N_LAYERS_A, A_NCOLS), 0.0, 1.0),
        'a_w0': uni((N_LAYERS_A, A_WIDTH), -4.0, 0.0),
        'a_w2': nrm((N_LAYERS_A, A_LORA_W, A_WIDTH), 0.5 * A_LORA_W ** -0.5),
        'a_a0': nrm((N_LAYERS_A, A_WIDTH), 0.5),
        'a_a2': nrm((N_LAYERS_A, A_LORA_A, A_WIDTH), 0.5 * A_LORA_A ** -0.5),
        'a_k_k': 0.85 + nrm((N_LAYERS_A, A_WIDTH), 0.05),
        'a_k_a': 1.0 + nrm((N_LAYERS_A, A_WIDTH), 0.05),
        'a_r_k': nrm((N_LAYERS_A, A_HEADS, A_HEAD_DIM), 0.1),
        'a_lnx_g': 1.0 + nrm((N_LAYERS_A, A_WIDTH), 0.05),
        'a_lnx_b': nrm((N_LAYERS_A, A_WIDTH), 0.02),
        'a_w_out': nrm((N_LAYERS_A, A_WIDTH, D_MODEL), DEEPNORM_BETA * A_WIDTH ** -0.5),
        'b_w_in': nrm((N_LAYERS_B, D_MODEL, 2 * B_WIDTH), D_MODEL ** -0.5),
        'b_w_grp': nrm((N_LAYERS_B, B_GROUPS, B_GROUP_W, B_GROUP_W), B_GROUP_W ** -0.5),
        'b_scale': 1.0 + nrm((N_LAYERS_B, B_WIDTH), 0.1),
        'b_w_out': nrm((N_LAYERS_B, B_WIDTH, D_MODEL), DEEPNORM_BETA * B_WIDTH ** -0.5),
        'c_w_in': nrm((N_LAYERS_C, D_MODEL, C_NCOLS), D_MODEL ** -0.5),
        'c_cmp_wk': (1.0 + nrm((N_LAYERS_C, C_CMP_BLOCK), 0.1)) / C_CMP_BLOCK,
        'c_cmp_wv': (1.0 + nrm((N_LAYERS_C, C_CMP_BLOCK), 0.1)) / C_CMP_BLOCK,
        'c_w_out': nrm((N_LAYERS_C, C_WIDTH, D_MODEL), DEEPNORM_BETA * C_WIDTH ** -0.5),
    }


def reference(x_prompt, x_sample, state_rwkv_S, state_rwkv_shift, state_pool,
              cache_cmp_k, cache_cmp_v, cache_sel_k, cache_sel_v, state_win_k, state_win_v, page_table,
              ln_g, ln_b, a_w_in, a_mu, a_w0, a_w2, a_a0, a_a2, a_k_k, a_k_a, a_r_k, a_lnx_g, a_lnx_b,
              a_w_out, b_w_in, b_w_grp, b_scale, b_w_out, c_w_in, c_cmp_wk, c_cmp_wv, c_w_out):
    Bp, Tp = x_prompt.shape[0], x_prompt.shape[1]
    xp, xs = x_prompt, x_sample
    S_p, S_s, sh_p, sh_s, pl_p, pl_s = [], [], [], [], [], []
    rows_p, rows_s, wk_p, wk_s, wv_p, wv_s = [], [], [], [], [], []
    for layer in range(DEPTH):
        kind, li = layer % N_MIXERS, layer // N_MIXERS
        if kind == 0:
            prm = (a_w_in[li], a_mu[li], a_w0[li], a_w2[li], a_a0[li], a_a2[li], a_k_k[li], a_k_a[li],
                   a_r_k[li], a_lnx_g[li], a_lnx_b[li], a_w_out[li])
            yp, s_new, sh_new = _rwkv_mixer(xp, jnp.zeros((Bp, A_HEADS, A_HEAD_DIM, A_HEAD_DIM), F32),
                                            jnp.zeros((Bp, A_NCOLS), xp.dtype), *prm)
            S_p.append(s_new)
            sh_p.append(sh_new)
            ys, s_new, sh_new = _rwkv_mixer(xs, state_rwkv_S[li], state_rwkv_shift[li], *prm)
            S_s.append(s_new)
            sh_s.append(sh_new)
        elif kind == 1:
            prm = (b_w_in[li], b_w_grp[li], b_scale[li], b_w_out[li])
            yp, buf_new = _pool_mixer(xp, jnp.zeros((Bp, B_BUF, B_WIDTH), xp.dtype), False, *prm)
            pl_p.append(buf_new)
            ys, buf_new = _pool_mixer(xs, state_pool[li], True, *prm)
            pl_s.append(buf_new)
        else:
            prm = (c_w_in[li], c_cmp_wk[li], c_cmp_wv[li], c_w_out[li])
            zero_win = jnp.zeros((Bp, C_WINDOW, C_KV_HEADS, C_HEAD_DIM), xp.dtype)
            yp, r_new, wk_new, wv_new = _nsa_mixer(xp, None, zero_win, zero_win, 0, min(C_WINDOW, Tp), *prm)
            rows_p.append(r_new)
            wk_p.append(wk_new)
            wv_p.append(wv_new)
            past = [_gather_pages(c[li], page_table) for c in (cache_cmp_k, cache_cmp_v, cache_sel_k, cache_sel_v)]
            ys, r_new, wk_new, wv_new = _nsa_mixer(xs, past, state_win_k[li], state_win_v[li], PAST_LEN,
                                                   state_win_k.shape[2], *prm)
            rows_s.append(r_new)
            wk_s.append(wk_new)
            wv_s.append(wv_new)
        xp = _layer_norm(DEEPNORM_ALPHA * xp + yp, ln_g[layer], ln_b[layer])
        xs = _layer_norm(DEEPNORM_ALPHA * xs + ys, ln_g[layer], ln_b[layer])
    rwkv_S_prompt = jnp.stack(S_p)
    rwkv_S_sample = jnp.stack(S_s)
    rwkv_shift_prompt = jnp.stack(sh_p)
    rwkv_shift_sample = jnp.stack(sh_s)
    pool_buf_prompt = jnp.stack(pl_p)
    pool_buf_sample = jnp.stack(pl_s)
    cmp_k_prompt = jnp.stack([r[0] for r in rows_p])
    cmp_k_sample = jnp.stack([r[0] for r in rows_s])
    cmp_v_prompt = jnp.stack([r[1] for r in rows_p])
    cmp_v_sample = jnp.stack([r[1] for r in rows_s])
    sel_k_prompt = jnp.stack([r[2] for r in rows_p])
    sel_k_sample = jnp.stack([r[2] for r in rows_s])
    sel_v_prompt = jnp.stack([r[3] for r in rows_p])
    sel_v_sample = jnp.stack([r[3] for r in rows_s])
    win_k_prompt = jnp.stack(wk_p)
    win_k_sample = jnp.stack(wk_s)
    win_v_prompt = jnp.stack(wv_p)
    win_v_sample = jnp.stack(wv_s)
    return (xp, xs, rwkv_S_prompt, rwkv_S_sample, rwkv_shift_prompt, rwkv_shift_sample,
            pool_buf_prompt, pool_buf_sample, cmp_k_prompt, cmp_k_sample, cmp_v_prompt, cmp_v_sample,
            sel_k_prompt, sel_k_sample, sel_v_prompt, sel_v_sample,
            win_k_prompt, win_k_sample, win_v_prompt, win_v_sample)
```

```python
import functools

import jax
import jax.numpy as jnp
import numpy as np
from jax import lax
from jax.experimental import pallas as pl
from jax.experimental.pallas import tpu as pltpu

F32 = jnp.float32
BF16 = jnp.bfloat16

D_MODEL = 1024
DEPTH = 4
DEEPNORM_ALPHA = (2.0 * DEPTH) ** 0.25
LN_EPS = 1e-5

A_HEADS = 16
A_HEAD_DIM = 64
A_WIDTH = 1024
A_LORA = 64
A_NCOLS = 4 * A_WIDTH + 2 * A_LORA
A_GN_EPS = 64e-5

B_WIDTH = 1024
B_GROUP_W = 256
B_WINDOWS = (2, 4, 8, 16)
B_BUF = 15
B_CARRY = 16

C_HEADS = 16
C_KV_HEADS = 4
C_HPG = 4
C_HEAD_DIM = 64
C_WIDTH = 1024
C_KV_WIDTH = 256
C_CMP_BLOCK = 32
C_SEL_BLOCK = 64
C_TOP_N = 16
C_WINDOW = 512
C_FORCE = 1e9
PAGE_SIZE = 128

LANES = 128
VMEM_LIMIT = 56 * 1024 * 1024


def _cparams(n_axes):
    return pltpu.CompilerParams(dimension_semantics=("arbitrary",) * n_axes, vmem_limit_bytes=VMEM_LIMIT)


def _sigmoid(x):
    return 1.0 / (1.0 + jnp.exp(-x))


def _silu(x):
    return x * _sigmoid(x)


def _proj_kernel(x_ref, w_ref, o_ref):
    o_ref[...] = jnp.dot(x_ref[...].astype(BF16), w_ref[...], preferred_element_type=F32)


def _proj(x, w, tm):
    m, k = x.shape
    n = w.shape[1]
    return pl.pallas_call(
        _proj_kernel,
        grid=(m // tm,),
        in_specs=[pl.BlockSpec((tm, k), lambda i: (i, 0)), pl.BlockSpec((k, n), lambda i: (0, 0))],
        out_specs=pl.BlockSpec((tm, n), lambda i: (i, 0)),
        out_shape=jax.ShapeDtypeStruct((m, n), F32),
        compiler_params=_cparams(1),
        name="proj",
    )(x, w)


def _outproj_ln_kernel(y_ref, z_ref, x_ref, w_ref, g_ref, b_ref, o_ref):
    a = (y_ref[...] * _silu(z_ref[...])).astype(BF16)
    h = DEEPNORM_ALPHA * x_ref[...] + jnp.dot(a, w_ref[...], preferred_element_type=F32)
    mu = jnp.mean(h, axis=-1, keepdims=True)
    c = h - mu
    var = jnp.mean(c * c, axis=-1, keepdims=True)
    o_ref[...] = c * lax.rsqrt(var + LN_EPS) * g_ref[...] + b_ref[...]


def _outproj_ln(y, z_arr, z_col, x, w, g, b, tm):
    m = x.shape[0]
    row = lambda i: (i, 0)
    const = lambda i: (0, 0)
    return pl.pallas_call(
        _outproj_ln_kernel,
        grid=(m // tm,),
        in_specs=[
            pl.BlockSpec((tm, D_MODEL), row),
            pl.BlockSpec((tm, D_MODEL), lambda i: (i, z_col)),
            pl.BlockSpec((tm, D_MODEL), row),
            pl.BlockSpec((D_MODEL, D_MODEL), const),
            pl.BlockSpec((1, D_MODEL), const),
            pl.BlockSpec((1, D_MODEL), const),
        ],
        out_specs=pl.BlockSpec((tm, D_MODEL), row),
        out_shape=jax.ShapeDtypeStruct((m, D_MODEL), F32),
        compiler_params=_cparams(1),
        name="outproj_ln",
    )(y, z_arr, x, w, g.reshape(1, D_MODEL), b.reshape(1, D_MODEL))


def _rwkv_prep_kernel(p_ref, prev_ref, mu_ref, w0_ref, a0_ref, lora_ref, kk_ref, ka_ref,
                      r_o, w_o, k_o, v_o, kk_o, a_o, z_o, carry, *, tt):
    t = pl.program_id(1)

    @pl.when(t == 0)
    def _():
        carry[0:1, :] = prev_ref[0]

    p = p_ref[...]
    first = lax.broadcasted_iota(jnp.int32, (tt, 1), 0) == 0
    p_shift = jnp.where(first, carry[0:1, :], pltpu.roll(p, 1, axis=0))
    carry[0:1, :] = p[tt - 1:tt, :]
    pm = p + (p_shift - p) * mu_ref[...]
    r = pm[:, 0:A_WIDTH]
    k = pm[:, A_WIDTH:2 * A_WIDTH]
    v = pm[:, 2 * A_WIDTH:3 * A_WIDTH]
    z = pm[:, 3 * A_WIDTH:4 * A_WIDTH]
    lo = pm[:, 4 * A_WIDTH:A_NCOLS]
    is_w = lax.broadcasted_iota(jnp.int32, (1, 2 * A_LORA), 1) < A_LORA
    lo = jnp.where(is_w, jnp.tanh(lo), lo)
    lora = jnp.dot(lo.astype(BF16), lora_ref[...], preferred_element_type=F32)
    u = w0_ref[...] + lora[:, 0:A_WIDTH]
    w_log = jnp.minimum(u, 0.0) - jnp.log(1.0 + jnp.exp(-jnp.abs(u))) - 0.5
    a = _sigmoid(a0_ref[...] + lora[:, A_WIDTH:2 * A_WIDTH])
    r_o[...] = r
    w_o[...] = jnp.exp(-jnp.exp(w_log))
    k_o[...] = k * (1.0 + (a - 1.0) * ka_ref[...])
    v_o[...] = v
    kk_o[...] = k * kk_ref[...]
    a_o[...] = a
    z_o[...] = z


def _rwkv_prep(p, p_prev, mu, w0, a0, lora_w, k_k, k_a, batch, seq, tt):
    nt = seq // tt
    row = lambda b, t: (b * nt + t, 0)
    const = lambda b, t: (0, 0)
    vec = lambda a: a.reshape(1, -1)
    out = jax.ShapeDtypeStruct((batch * seq, A_WIDTH), F32)
    return pl.pallas_call(
        functools.partial(_rwkv_prep_kernel, tt=tt),
        grid=(batch, nt),
        in_specs=[
            pl.BlockSpec((tt, A_NCOLS), row),
            pl.BlockSpec((1, 1, A_NCOLS), lambda b, t: (b, 0, 0)),
            pl.BlockSpec((1, A_NCOLS), const),
            pl.BlockSpec((1, A_WIDTH), const),
            pl.BlockSpec((1, A_WIDTH), const),
            pl.BlockSpec((2 * A_LORA, 2 * A_WIDTH), const),
            pl.BlockSpec((1, A_WIDTH), const),
            pl.BlockSpec((1, A_WIDTH), const),
        ],
        out_specs=[pl.BlockSpec((tt, A_WIDTH), row)] * 7,
        out_shape=[out] * 7,
        scratch_shapes=[pltpu.VMEM((8, A_NCOLS), F32)],
        compiler_params=_cparams(2),
        name="rwkv_prep",
    )(p, p_prev.reshape(batch, 1, A_NCOLS), vec(mu), vec(w0), vec(a0), lora_w, vec(k_k), vec(k_a))


def _rwkv_scan_kernel(r_ref, w_ref, k_ref, v_ref, kkr_ref, a_ref, s0_ref, rk_ref, lg_ref, lb_ref,
                      y_ref, sout_ref, state, kk_s, b_s, wr_s, *, tt):
    t = pl.program_id(1)
    n = A_HEAD_DIM

    @pl.when(t == 0)
    def _():
        state[...] = s0_ref[...]

    kkr = kkr_ref[...]
    norm2 = jnp.sum(kkr * kkr, axis=1, keepdims=True)
    kk = kkr * lax.rsqrt(jnp.maximum(norm2, 1e-24))
    kk_s[...] = kk
    b_s[...] = kk * a_ref[...]
    wr_s[...] = w_ref[...] * r_ref[...]
    rk = rk_ref[...]
    lg = lg_ref[...]
    lb = lb_ref[...]

    def step(s, carry):
        sa = jnp.zeros((n, LANES), F32)
        y = jnp.zeros((n, LANES), F32)
        for j in range(n):
            sj = state[j]
            sa = sa - sj * kk_s[s, j:j + 1, :]
            y = y + sj * wr_s[s, j:j + 1, :]
        v = v_ref[s]
        for j in range(n):
            state[j] = state[j] * w_ref[s, j:j + 1, :] + sa * b_s[s, j:j + 1, :] + v * k_ref[s, j:j + 1, :]
        r = r_ref[s]
        k = k_ref[s]
        b_r = jnp.sum(b_s[s] * r, axis=0, keepdims=True)
        k_r = jnp.sum(k * r, axis=0, keepdims=True)
        bonus = jnp.sum(k * r * rk, axis=0, keepdims=True)
        y = y + sa * b_r + v * k_r
        mean = jnp.mean(y, axis=0, keepdims=True)
        c = y - mean
        var = jnp.mean(c * c, axis=0, keepdims=True)
        y_ref[s] = c * lax.rsqrt(var + A_GN_EPS) * lg + lb + bonus * v
        return carry

    lax.fori_loop(0, tt, step, 0)

    @pl.when(t == pl.num_programs(1) - 1)
    def _():
        sout_ref[...] = state[...]


def _rwkv_scan(r, w, k, v, kkr, a, s0, rk, lg, lb, tt):
    seq, n, chains = r.shape
    seq_blk = pl.BlockSpec((tt, n, LANES), lambda c, t: (t, 0, c))
    st_blk = pl.BlockSpec((n, n, LANES), lambda c, t: (0, 0, c))
    par_blk = pl.BlockSpec((n, LANES), lambda c, t: (0, c))
    return pl.pallas_call(
        functools.partial(_rwkv_scan_kernel, tt=tt),
        grid=(chains // LANES, seq // tt),
        in_specs=[seq_blk] * 6 + [st_blk] + [par_blk] * 3,
        out_specs=[seq_blk, st_blk],
        out_shape=[jax.ShapeDtypeStruct((seq, n, chains), F32), jax.ShapeDtypeStruct((n, n, chains), F32)],
        scratch_shapes=[pltpu.VMEM((n, n, LANES), F32)] + [pltpu.VMEM((tt, n, LANES), F32)] * 3,
        compiler_params=_cparams(2),
        name="rwkv_scan",
    )(r, w, k, v, kkr, a, s0, rk, lg, lb)


def _rwkv_layer(x, batch, seq, s0, p_prev, prm, ln_g, ln_b, tm, tt_prep, tt_scan):
    (w_in, mu, w0, w2, a0, a2, k_k, k_a, r_k, lnx_g, lnx_b, w_out) = prm
    h, n = A_HEADS, A_HEAD_DIM
    p = _proj(x, w_in.astype(BF16), tm)
    zeros = jnp.zeros((A_LORA, A_WIDTH), F32)
    lora_w = jnp.concatenate([jnp.concatenate([w2, zeros], axis=1), jnp.concatenate([zeros, a2], axis=1)], axis=0)
    r, w, k, v, kkr, a, z = _rwkv_prep(p, p_prev, mu, w0, a0, lora_w.astype(BF16), k_k, k_a, batch, seq, tt_prep)
    chains = batch * h
    to_scan = lambda t: t.reshape(batch, seq, h, n).transpose(1, 3, 0, 2).reshape(seq, n, chains)
    per_chain = lambda t: jnp.broadcast_to(t.reshape(h, n).T[:, None, :], (n, batch, h)).reshape(n, chains)
    s0_t = s0.transpose(3, 2, 0, 1).reshape(n, n, chains)
    y, s_out = _rwkv_scan(to_scan(r), to_scan(w), to_scan(k), to_scan(v), to_scan(kkr), to_scan(a), s0_t,
                          per_chain(r_k), per_chain(lnx_g), per_chain(lnx_b), tt_scan)
    y = y.reshape(seq, n, batch, h).transpose(2, 0, 3, 1).reshape(batch * seq, A_WIDTH)
    x_new = _outproj_ln(y, z, 0, x, w_out.astype(BF16), ln_g, ln_b, tm)
    s_final = s_out.reshape(n, n, batch, h).transpose(2, 3, 1, 0)
    p_last = p.reshape(batch, seq, A_NCOLS)[:, -1]
    return x_new, s_final, p_last


def _pool_kernel(u_ref, buf_ref, wg_ref, scale_ref, y_ref, nbuf_ref, ext, *, tt, front_valid):
    t = pl.program_id(1)

    @pl.when(t == 0)
    def _():
        ext[1:B_CARRY, :] = buf_ref[0]

    @pl.when(t > 0)
    def _():
        ext[0:B_CARRY, :] = ext[tt:tt + B_CARRY, :]

    u = u_ref[...]
    ext[B_CARRY:B_CARRY + tt, :] = u
    pos = t * tt + lax.broadcasted_iota(jnp.int32, (tt, 1), 0)
    for gi, win in enumerate(B_WINDOWS):
        lo, hi = gi * B_GROUP_W, (gi + 1) * B_GROUP_W
        ug = u[:, lo:hi]
        acc = ug
        for s in range(1, win):
            acc = acc + ext[B_CARRY - s:B_CARRY - s + tt, lo:hi]
        if front_valid:
            cnt = float(win)
        else:
            cnt = jnp.minimum(pos + 1, win).astype(F32)
        d = acc / cnt - ug
        yg = jnp.dot(d.astype(BF16), wg_ref[gi], preferred_element_type=F32)
        y_ref[:, lo:hi] = yg * scale_ref[:, lo:hi]

    @pl.when(t == pl.num_programs(1) - 1)
    def _():
        nbuf_ref[0] = ext[tt + 1:tt + B_CARRY, :]


def _pool(uz, buf, w_grp, scale, batch, seq, tt, front_valid):
    nt = seq // tt
    return pl.pallas_call(
        functools.partial(_pool_kernel, tt=tt, front_valid=front_valid),
        grid=(batch, nt),
        in_specs=[
            pl.BlockSpec((tt, B_WIDTH), lambda b, t: (b * nt + t, 0)),
            pl.BlockSpec((1, B_BUF, B_WIDTH), lambda b, t: (b, 0, 0)),
            pl.BlockSpec((len(B_WINDOWS), B_GROUP_W, B_GROUP_W), lambda b, t: (0, 0, 0)),
            pl.BlockSpec((1, B_WIDTH), lambda b, t: (0, 0)),
        ],
        out_specs=[
            pl.BlockSpec((tt, B_WIDTH), lambda b, t: (b * nt + t, 0)),
            pl.BlockSpec((1, B_BUF, B_WIDTH), lambda b, t: (b, 0, 0)),
        ],
        out_shape=[jax.ShapeDtypeStruct((batch * seq, B_WIDTH), F32),
                   jax.ShapeDtypeStruct((batch, B_BUF, B_WIDTH), F32)],
        scratch_shapes=[pltpu.VMEM((tt + B_CARRY, B_WIDTH), F32)],
        compiler_params=_cparams(2),
        name="pool",
    )(uz, buf, w_grp, scale.reshape(1, B_WIDTH))


def _pool_layer(x, batch, seq, buf, front_valid, prm, ln_g, ln_b, tm, tt):
    w_in, w_grp, scale, w_out = prm
    uz = _proj(x, w_in.astype(BF16), tm)
    y, new_buf = _pool(uz, buf, w_grp.astype(BF16), scale, batch, seq, tt, front_valid)
    x_new = _outproj_ln(y, uz, 1, x, w_out.astype(BF16), ln_g, ln_b, tm)
    return x_new, new_buf


NEG = -1e30
C_SCALE = C_HEAD_DIM ** -0.5


def _alibi_slopes():
    return jnp.power(2.0, -8.0 * (jnp.arange(C_HEADS, dtype=F32) + 1.0) / C_HEADS)


def _dot_t(a, b):
    return lax.dot_general(a, b, (((1,), (1,)), ((), ())), preferred_element_type=F32)


def _masked_softmax_parts(scores, masks):
    m = None
    for s, k in zip(scores, masks):
        part = jnp.max(jnp.where(k, s, -jnp.inf), axis=1, keepdims=True)
        m = part if m is None else jnp.maximum(m, part)
    m = jnp.where(jnp.isfinite(m), m, 0.0)
    es = [jnp.where(k, jnp.exp(s - m), 0.0) for s, k in zip(scores, masks)]
    den = None
    for e in es:
        part = jnp.sum(e, axis=1, keepdims=True)
        den = part if den is None else den + part
    den = jnp.maximum(den, 1e-30)
    return [e / den for e in es]


def _flash_init(rows, width):
    return (jnp.full((rows, 1), NEG, F32), jnp.zeros((rows, 1), F32), jnp.zeros((rows, width), F32))


def _flash_update(carry, qb, kb, vb, bias):
    m, l, acc = carry
    s, mask = bias(_dot_t(qb, kb))
    s = jnp.where(mask, s, NEG)
    m_new = jnp.maximum(m, jnp.max(s, axis=1, keepdims=True))
    alpha = jnp.exp(m - m_new)
    p = jnp.where(mask, jnp.exp(s - m_new), 0.0)
    l = alpha * l + jnp.sum(p, axis=1, keepdims=True)
    acc = alpha * acc + jnp.dot(p.astype(BF16), vb, preferred_element_type=F32)
    return m_new, l, acc


def _flash_finish(carry):
    _, l, acc = carry
    return acc / jnp.maximum(l, 1e-30)


def _top_n_mask(imp, top_n):
    n = imp.shape[1]
    idx = lax.broadcasted_iota(jnp.int32, (1, n), 1)
    rank = jnp.zeros(imp.shape, F32)
    for c in range(n):
        col = imp[:, c:c + 1]
        rank = rank + jnp.where(idx > c, jnp.where(col >= imp, 1.0, 0.0), jnp.where(col > imp, 1.0, 0.0))
    return jnp.where(rank < top_n, jnp.where(imp >= 0.0, 1.0, 0.0), 0.0)


def _block_expand(n_blocks, kpos):
    blk = lax.broadcasted_iota(jnp.int32, (n_blocks, kpos.shape[1]), 0)
    return jnp.where(lax.shift_right_logical(kpos, 6) == blk, 1.0, 0.0).astype(BF16)


def _cmp_kernel(kv_ref, w_ref, e_ref, o_ref):
    x = kv_ref[...]
    n_pair = x.shape[0] // (2 * C_CMP_BLOCK)
    x4 = x.reshape(n_pair, 2, C_CMP_BLOCK, x.shape[1])
    w = w_ref[...][None]
    e_ref[0] = jnp.sum(x4[:, 0] * w, axis=1)
    o_ref[0] = jnp.sum(x4[:, 1] * w, axis=1)


def _cmp(p, col_block, w, batch, seq):
    width = w.shape[1]
    n_pair = seq // (2 * C_CMP_BLOCK)
    out = jax.ShapeDtypeStruct((batch, n_pair, width), F32)
    return pl.pallas_call(
        _cmp_kernel,
        grid=(batch,),
        in_specs=[pl.BlockSpec((seq, width), lambda b: (b, col_block)), pl.BlockSpec(w.shape, lambda b: (0, 0))],
        out_specs=[pl.BlockSpec((1, n_pair, width), lambda b: (b, 0, 0))] * 2,
        out_shape=[out, out],
        compiler_params=_cparams(1),
        name="nsa_cmp",
    )(p, w)


def _nsa_prompt_kernel(q_ref, g_ref, ce_ref, co_ref, sel_ref, win_ref, slope_ref, o_ref, *, tq, tk):
    q0 = pl.program_id(2) * tq
    hd = C_HEAD_DIM
    rows = C_HPG * tq
    n_pair = ce_ref.shape[1]
    stack = lambda f: jnp.concatenate([f(j) for j in range(C_HPG)], axis=0)
    q_all = q_ref[...]
    qb = (stack(lambda j: q_all[:, j * hd:(j + 1) * hd]) * C_SCALE).astype(BF16)
    slope = stack(lambda j: jnp.broadcast_to(slope_ref[0, j:j + 1, 0:1], (tq, 1)))
    qpos_t = q0 + lax.broadcasted_iota(jnp.int32, (tq, 1), 0)
    qpos = stack(lambda j: qpos_t)
    gates = _sigmoid(g_ref[...])
    gate = lambda br: stack(lambda j: gates[:, 3 * j + br:3 * j + br + 1])

    pair = lax.broadcasted_iota(jnp.int32, (1, n_pair), 1)
    ce, co = ce_ref[0], co_ref[0]
    scores, masks = [], []
    for par, c in ((0, ce), (1, co)):
        dist = qpos - ((2 * pair + par + 1) * C_CMP_BLOCK - 1)
        scores.append(_dot_t(qb, c[:, :hd].astype(BF16)) - slope * dist.astype(F32))
        masks.append(dist >= 0)
    p_e, p_o = _masked_softmax_parts(scores, masks)
    o_c = (jnp.dot(p_e.astype(BF16), ce[:, hd:].astype(BF16), preferred_element_type=F32)
           + jnp.dot(p_o.astype(BF16), co[:, hd:].astype(BF16), preferred_element_type=F32))
    imp_h = p_e + p_o
    imp = imp_h[0:tq]
    for j in range(1, C_HPG):
        imp = imp + imp_h[j * tq:(j + 1) * tq]
    cur = lax.shift_right_logical(qpos_t, 6)
    imp = jnp.where(pair == cur, C_FORCE, imp)
    imp = jnp.where(pair <= cur, imp, -1.0)
    sel = _top_n_mask(imp, C_TOP_N)
    sel_rows = stack(lambda j: sel).astype(BF16)

    def sel_body(c, carry):
        k0 = pl.multiple_of(c * tk, tk)
        kv = sel_ref[pl.ds(k0, tk), :]
        kpos = k0 + lax.broadcasted_iota(jnp.int32, (1, tk), 1)
        chosen = jnp.dot(sel_rows, _block_expand(n_pair, kpos), preferred_element_type=F32)

        def bias(s):
            dist = qpos - kpos
            return s - slope * dist.astype(F32), jnp.where(dist >= 0, chosen, 0.0) > 0.5

        return _flash_update(carry, qb, kv[:, :hd].astype(BF16), kv[:, hd:].astype(BF16), bias)

    o_s = _flash_finish(lax.fori_loop(0, (q0 + tq) // tk, sel_body, _flash_init(rows, hd)))

    w0 = jnp.maximum(q0 - C_WINDOW, 0)

    def win_body(c, carry):
        k0 = pl.multiple_of(w0 + c * tk, tk)
        kv = win_ref[pl.ds(k0, tk), :]
        kpos = k0 + lax.broadcasted_iota(jnp.int32, (1, tk), 1)

        def bias(s):
            dist = qpos - kpos
            return s - slope * dist.astype(F32), jnp.where(dist >= 0, dist, C_WINDOW) < C_WINDOW

        return _flash_update(carry, qb, kv[:, :hd].astype(BF16), kv[:, hd:].astype(BF16), bias)

    o_w = _flash_finish(lax.fori_loop(0, (tq + C_WINDOW) // tk, win_body, _flash_init(rows, hd)))

    o = gate(0) * o_c + gate(1) * o_s + gate(2) * o_w
    o_ref[...] = jnp.concatenate([o[j * tq:(j + 1) * tq] for j in range(C_HPG)], axis=1)


def _nsa_prompt_layer(x, batch, seq, prm, ln_g, ln_b):
    w_in, cmp_wk, cmp_wv, w_out = prm
    hd, grp = C_HEAD_DIM, C_KV_HEADS
    tq = tk = 256
    assert seq % tq == 0 and seq >= tq + C_WINDOW and C_WINDOW % tk == 0 and seq % (2 * C_CMP_BLOCK) == 0
    kv_w = lambda i: w_in[:, C_WIDTH + C_KV_WIDTH * i:C_WIDTH + C_KV_WIDTH * (i + 1)].reshape(D_MODEL, grp, hd)
    kv_pair = lambda a, b: jnp.concatenate([kv_w(a), kv_w(b)], axis=2).reshape(D_MODEL, grp * 2 * hd)
    g_lo = C_WIDTH + 6 * C_KV_WIDTH
    g_w = w_in[:, g_lo:g_lo + 3 * C_HEADS].reshape(D_MODEL, grp, 3 * C_HPG)
    g_w = jnp.pad(g_w, ((0, 0), (0, 0), (0, LANES - 3 * C_HPG))).reshape(D_MODEL, grp * LANES)
    w_p = jnp.concatenate([w_in[:, :C_WIDTH], w_in[:, g_lo + 3 * C_HEADS:], kv_pair(0, 1), kv_pair(2, 3),
                           kv_pair(4, 5), g_w], axis=1)
    p = _proj(x, w_p.astype(BF16), 512)
    kv_lo = 2 * C_WIDTH
    kvw = grp * 2 * hd
    cw = jnp.concatenate([jnp.broadcast_to(cmp_wk[:, None], (C_CMP_BLOCK, hd)),
                          jnp.broadcast_to(cmp_wv[:, None], (C_CMP_BLOCK, hd))], axis=1)
    ce, co = _cmp(p, kv_lo // kvw, jnp.tile(cw, (1, grp)), batch, seq)
    slopes = jnp.broadcast_to(jnp.pad(_alibi_slopes().reshape(grp, C_HPG), ((0, 0), (0, 8 - C_HPG)))[:, :, None],
                              (grp, 8, LANES))
    nq = seq // tq
    n_pair = seq // (2 * C_CMP_BLOCK)
    gw = 2 * hd
    o = pl.pallas_call(
        functools.partial(_nsa_prompt_kernel, tq=tq, tk=tk),
        grid=(batch, grp, nq),
        in_specs=[
            pl.BlockSpec((tq, C_HPG * hd), lambda b, g, i: (b * nq + i, g)),
            pl.BlockSpec((tq, LANES), lambda b, g, i: (b * nq + i, (kv_lo + 3 * kvw) // LANES + g)),
            pl.BlockSpec((1, n_pair, gw), lambda b, g, i: (b, 0, g)),
            pl.BlockSpec((1, n_pair, gw), lambda b, g, i: (b, 0, g)),
            pl.BlockSpec((seq, gw), lambda b, g, i: (b, (kv_lo + kvw) // gw + g)),
            pl.BlockSpec((seq, gw), lambda b, g, i: (b, (kv_lo + 2 * kvw) // gw + g)),
            pl.BlockSpec((1, 8, LANES), lambda b, g, i: (g, 0, 0)),
        ],
        out_specs=pl.BlockSpec((tq, C_HPG * hd), lambda b, g, i: (b * nq + i, g)),
        out_shape=jax.ShapeDtypeStruct((batch * seq, C_WIDTH), F32),
        compiler_params=_cparams(3),
        name="nsa_prompt",
    )(p, p, ce, co, p, p, slopes)
    x_new = _outproj_ln(o, p, 1, x, w_out.astype(BF16), ln_g, ln_b, 512)
    branch = lambda i: p[:, kv_lo + i * kvw:kv_lo + (i + 1) * kvw].reshape(batch, seq, grp, 2, hd)
    rows = (branch(0)[:, :, :, 0], branch(0)[:, :, :, 1], branch(1)[:, :, :, 0], branch(1)[:, :, :, 1])
    keep = min(C_WINDOW, seq)
    return x_new, rows, branch(2)[:, seq - keep:, :, 0], branch(2)[:, seq - keep:, :, 1]


def _nsa_sample_kernel(tbl_ref, *refs, n_pages, ts):
    del tbl_ref
    ck, cv, sk, sv = (refs[i * n_pages:(i + 1) * n_pages] for i in range(4))
    (q_ref, g_ref, ksn_ref, vsn_ref, kwn_ref, vwn_ref, wk_ref, wv_ref, cwk_ref, cwv_ref, slope_ref,
     o_ref, ke_s, ko_s, ve_s, vo_s, new_s) = refs[4 * n_pages:]
    hd, grp = C_HEAD_DIM, C_KV_HEADS
    rows = C_HEADS * ts
    past = n_pages * PAGE_SIZE
    n_pair = past // C_SEL_BLOCK
    qpos = past + lax.rem(lax.broadcasted_iota(jnp.int32, (rows, 1), 0), ts)
    qb = (q_ref[0] * C_SCALE).astype(BF16)
    slope = slope_ref[:, 0:1]

    per_page = PAGE_SIZE // C_CMP_BLOCK
    for pages, cw_ref, e_s, o_s in ((ck, cwk_ref, ke_s, ko_s), (cv, cwv_ref, ve_s, vo_s)):
        cw = cw_ref[...][None]
        for pg in range(n_pages):
            tok = jnp.sum(pages[pg][0].reshape(per_page, C_CMP_BLOCK, grp * hd) * cw, axis=1)
            for i in range(per_page):
                n = pg * per_page + i
                dst = e_s if n % 2 == 0 else o_s
                dst[n // 2:n // 2 + 1, :] = tok[i:i + 1]

    pair = lax.broadcasted_iota(jnp.int32, (1, n_pair), 1)
    scores, masks = [], []
    for par, k_s in ((0, ke_s), (1, ko_s)):
        dist = qpos - ((2 * pair + par + 1) * C_CMP_BLOCK - 1)
        scores.append(_dot_t(qb, k_s[...].astype(BF16)) - slope * dist.astype(F32))
        masks.append(dist >= 0)
    p_e, p_o = _masked_softmax_parts(scores, masks)
    o_c = (jnp.dot(p_e.astype(BF16), ve_s[...].astype(BF16), preferred_element_type=F32)
           + jnp.dot(p_o.astype(BF16), vo_s[...].astype(BF16), preferred_element_type=F32))
    imp_h = p_e + p_o
    sel_rows = []
    for g in range(grp):
        base = g * C_HPG * ts
        imp = imp_h[base:base + ts]
        for j in range(1, C_HPG):
            imp = imp + imp_h[base + j * ts:base + (j + 1) * ts]
        sel_g = _top_n_mask(imp, C_TOP_N - 1)
        sel_rows += [sel_g] * C_HPG
    sel_rows = jnp.concatenate(sel_rows, axis=0).astype(BF16)

    lane_pos = lax.broadcasted_iota(jnp.int32, (1, PAGE_SIZE), 1)

    def new_rows(k_ref, v_ref):
        new_s[...] = jnp.zeros(new_s.shape, F32)
        new_s[0, 0:ts, :] = k_ref[...]
        new_s[1, 0:ts, :] = v_ref[...]
        return new_s[0].astype(BF16), new_s[1].astype(BF16)

    def causal(kpos):
        def bias(s):
            dist = qpos - kpos
            return s - slope * dist.astype(F32), dist >= 0
        return bias

    def window(kpos):
        def bias(s):
            dist = qpos - kpos
            return s - slope * dist.astype(F32), jnp.where(dist >= 0, dist, C_WINDOW) < C_WINDOW
        return bias

    carry = _flash_init(rows, grp * hd)
    for pg in range(n_pages):
        kpos = pg * PAGE_SIZE + lane_pos
        chosen = jnp.dot(sel_rows, _block_expand(n_pair, kpos), preferred_element_type=F32)

        def bias(s, kpos=kpos, chosen=chosen):
            dist = qpos - kpos
            return s - slope * dist.astype(F32), jnp.where(dist >= 0, chosen, 0.0) > 0.5

        carry = _flash_update(carry, qb, sk[pg][0].astype(BF16), sv[pg][0].astype(BF16), bias)
    kb, vb = new_rows(ksn_ref, vsn_ref)
    o_s = _flash_finish(_flash_update(carry, qb, kb, vb, causal(past + lane_pos)))

    carry = _flash_init(rows, grp * hd)
    n_win = wk_ref.shape[1]
    for c in range(n_win // PAGE_SIZE):
        lo = c * PAGE_SIZE
        carry = _flash_update(carry, qb, wk_ref[0, lo:lo + PAGE_SIZE, :].astype(BF16),
                              wv_ref[0, lo:lo + PAGE_SIZE, :].astype(BF16), window(past - n_win + lo + lane_pos))
    kb, vb = new_rows(kwn_ref, vwn_ref)
    o_w = _flash_finish(_flash_update(carry, qb, kb, vb, window(past + lane_pos)))

    per_group = C_HPG * ts
    diag = lambda full: jnp.concatenate(
        [full[g * per_group:(g + 1) * per_group, g * hd:(g + 1) * hd] for g in range(grp)], axis=0)
    gates = _sigmoid(g_ref[0])
    o_ref[0] = gates[:, 0:1] * diag(o_c) + gates[:, 1:2] * diag(o_s) + gates[:, 2:3] * diag(o_w)


def _nsa_sample_layer(x, batch, ts, caches, page_table, win_k, win_v, prm, ln_g, ln_b):
    w_in, cmp_wk, cmp_wv, w_out = prm
    hd, grp = C_HEAD_DIM, C_KV_HEADS
    n_pages = page_table.shape[1]
    n_win = win_k.shape[1]
    past = n_pages * PAGE_SIZE
    rows = C_HEADS * ts
    assert past % C_SEL_BLOCK == 0 and ts <= C_SEL_BLOCK and ts % 8 == 0 and n_win == C_WINDOW and past >= n_win
    g_lo = C_WIDTH + 6 * C_KV_WIDTH
    w_s = jnp.concatenate([w_in[:, :C_WIDTH], w_in[:, g_lo + 3 * C_HEADS:], w_in[:, C_WIDTH:g_lo],
                           jnp.pad(w_in[:, g_lo:g_lo + 3 * C_HEADS], ((0, 0), (0, LANES - 3 * C_HEADS)))], axis=1)
    p = _proj(x, w_s.astype(BF16), 512)
    kv_lo = 2 * C_WIDTH
    q = p[:, :C_WIDTH].reshape(batch, ts, grp, C_HPG, hd).transpose(0, 2, 3, 1, 4).reshape(batch, grp, C_HPG * ts, hd)
    q_bd = (q[:, :, :, None, :] * jnp.eye(grp, dtype=F32)[None, :, None, :, None]).reshape(batch, rows, grp * hd)
    g_lo_p = kv_lo + 6 * C_KV_WIDTH
    g_t = p[:, g_lo_p:g_lo_p + 3 * C_HEADS].reshape(batch, ts, C_HEADS, 3).transpose(0, 2, 1, 3).reshape(batch, rows, 3)
    slope_rows = jnp.broadcast_to(jnp.repeat(_alibi_slopes(), ts)[:, None], (rows, LANES))
    bcast = lambda w: jnp.broadcast_to(w[:, None], (C_CMP_BLOCK, grp * hd))
    pools = [c.reshape(c.shape[0], PAGE_SIZE, grp * hd) for c in caches]
    page_spec = lambda pg: pl.BlockSpec((1, PAGE_SIZE, grp * hd), lambda b, tbl: (tbl[b, pg], 0, 0))
    new_spec = lambda i: pl.BlockSpec((ts, C_KV_WIDTH), lambda b, tbl: (b, kv_lo // C_KV_WIDTH + i))
    const2 = lambda shape: pl.BlockSpec(shape, lambda b, tbl: (0, 0))
    per_b = lambda shape: pl.BlockSpec((1,) + shape, lambda b, tbl: (b, 0, 0))
    in_specs = ([page_spec(pg) for _ in range(4) for pg in range(n_pages)]
                + [per_b((rows, grp * hd)), per_b((rows, 3)), new_spec(2), new_spec(3), new_spec(4), new_spec(5),
                   per_b((n_win, grp * hd)), per_b((n_win, grp * hd)),
                   const2((C_CMP_BLOCK, grp * hd)), const2((C_CMP_BLOCK, grp * hd)), const2((rows, LANES))])
    n_pair = past // C_SEL_BLOCK
    o = pl.pallas_call(
        functools.partial(_nsa_sample_kernel, n_pages=n_pages, ts=ts),
        grid_spec=pltpu.PrefetchScalarGridSpec(
            num_scalar_prefetch=1,
            grid=(batch,),
            in_specs=in_specs,
            out_specs=per_b((rows, hd)),
            scratch_shapes=[pltpu.VMEM((n_pair, grp * hd), F32)] * 4 + [pltpu.VMEM((2, PAGE_SIZE, grp * hd), F32)],
        ),
        out_shape=jax.ShapeDtypeStruct((batch, rows, hd), F32),
        compiler_params=_cparams(1),
        name="nsa_sample",
    )(page_table, *[pool for pool in pools for _ in range(n_pages)], q_bd, g_t, p, p, p, p,
      win_k.reshape(batch, n_win, grp * hd), win_v.reshape(batch, n_win, grp * hd), bcast(cmp_wk), bcast(cmp_wv),
      slope_rows)
    y = o.reshape(batch, C_HEADS, ts, hd).transpose(0, 2, 1, 3).reshape(batch * ts, C_WIDTH)
    x_new = _outproj_ln(y, p, 1, x, w_out.astype(BF16), ln_g, ln_b, 512)
    new = lambda i: p[:, kv_lo + i * C_KV_WIDTH:kv_lo + (i + 1) * C_KV_WIDTH].reshape(batch, ts, grp, hd)
    keep = lambda buf, i: jnp.concatenate([buf, new(i)], axis=1)[:, -n_win:]
    return x_new, (new(0), new(1), new(2), new(3)), keep(win_k, 4), keep(win_v, 5)


def kernel(x_prompt, x_sample, state_rwkv_S, state_rwkv_shift, state_pool, cache_cmp_k, cache_cmp_v, cache_sel_k,
           cache_sel_v, state_win_k, state_win_v, page_table, ln_g, ln_b, a_w_in, a_mu, a_w0, a_w2, a_a0, a_a2, a_k_k,
           a_k_a, a_r_k, a_lnx_g, a_lnx_b, a_w_out, b_w_in, b_w_grp, b_scale, b_w_out, c_w_in, c_cmp_wk, c_cmp_wv,
           c_w_out):
    bp, tp, _ = x_prompt.shape
    bs, ts, _ = x_sample.shape
    xp = x_prompt.reshape(bp * tp, D_MODEL)
    xs = x_sample.reshape(bs * ts, D_MODEL)
    s_p, s_s, sh_p, sh_s, pl_p, pl_s = [], [], [], [], [], []
    rows_p, rows_s, wk_p, wk_s, wv_p, wv_s = [], [], [], [], [], []
    for layer in range(DEPTH):
        kind, li = layer % 3, layer // 3
        g, b = ln_g[layer], ln_b[layer]
        if kind == 0:
            prm = (a_w_in[li], a_mu[li], a_w0[li], a_w2[li], a_a0[li], a_a2[li], a_k_k[li], a_k_a[li], a_r_k[li],
                   a_lnx_g[li], a_lnx_b[li], a_w_out[li])
            xp, s_new, sh_new = _rwkv_layer(xp, bp, tp, jnp.zeros((bp, A_HEADS, A_HEAD_DIM, A_HEAD_DIM), F32),
                                            jnp.zeros((bp, A_NCOLS), F32), prm, g, b, 512, 256, 32)
            s_p.append(s_new)
            sh_p.append(sh_new)
            xs, s_new, sh_new = _rwkv_layer(xs, bs, ts, state_rwkv_S[li], state_rwkv_shift[li], prm, g, b, 512, ts, ts)
            s_s.append(s_new)
            sh_s.append(sh_new)
        elif kind == 1:
            prm = (b_w_in[li], b_w_grp[li], b_scale[li], b_w_out[li])
            xp, buf_new = _pool_layer(xp, bp, tp, jnp.zeros((bp, B_BUF, B_WIDTH), F32), False, prm, g, b, 512, 512)
            pl_p.append(buf_new)
            xs, buf_new = _pool_layer(xs, bs, ts, state_pool[li], True, prm, g, b, 512, ts)
            pl_s.append(buf_new)
        else:
            prm = (c_w_in[li], c_cmp_wk[li], c_cmp_wv[li], c_w_out[li])
            xp, rows, wk, wv = _nsa_prompt_layer(xp, bp, tp, prm, g, b)
            rows_p.append(rows)
            wk_p.append(wk)
            wv_p.append(wv)
            caches = (cache_cmp_k[li], cache_cmp_v[li], cache_sel_k[li], cache_sel_v[li])
            xs, rows, wk, wv = _nsa_sample_layer(xs, bs, ts, caches, page_table, state_win_k[li], state_win_v[li],
                                                 prm, g, b)
            rows_s.append(rows)
            wk_s.append(wk)
            wv_s.append(wv)
    stack = jnp.stack
    return (xp.reshape(bp, tp, D_MODEL), xs.reshape(bs, ts, D_MODEL), stack(s_p), stack(s_s), stack(sh_p), stack(sh_s),
            stack(pl_p), stack(pl_s),
            stack([r[0] for r in rows_p]), stack([r[0] for r in rows_s]),
            stack([r[1] for r in rows_p]), stack([r[1] for r in rows_s]),
            stack([r[2] for r in rows_p]), stack([r[2] for r in rows_s]),
            stack([r[3] for r in rows_p]), stack([r[3] for r in rows_s]),
            stack(wk_p), stack(wk_s), stack(wv_p), stack(wv_s))
```

```python
import functools

import jax
import jax.numpy as jnp
import numpy as np
from jax import lax
from jax.experimental import pallas as pl
from jax.experimental.pallas import tpu as pltpu

F32 = jnp.float32
BF16 = jnp.bfloat16

D_MODEL = 1024
DEPTH = 4
DEEPNORM_ALPHA = (2.0 * DEPTH) ** 0.25
LN_EPS = 1e-5

A_HEADS = 16
A_HEAD_DIM = 64
A_WIDTH = 1024
A_LORA = 64
A_NCOLS = 4 * A_WIDTH + 2 * A_LORA
A_GN_EPS = 64e-5
RWKV_CHUNK = 64
RWKV_ROWS_PER_STEP = 4

B_WIDTH = 1024
B_GROUP_W = 256
B_WINDOWS = (2, 4, 8, 16)
B_BUF = 15
B_CARRY = 16

C_HEADS = 16
C_KV_HEADS = 4
C_HPG = 4
C_HEAD_DIM = 64
C_WIDTH = 1024
C_KV_WIDTH = 256
C_CMP_BLOCK = 32
C_SEL_BLOCK = 64
C_TOP_N = 16
C_WINDOW = 512
C_FORCE = 1e9
PAGE_SIZE = 128

LANES = 128
VMEM_LIMIT = 56 * 1024 * 1024


def _cparams(n_axes):
    return pltpu.CompilerParams(dimension_semantics=("arbitrary",) * n_axes, vmem_limit_bytes=VMEM_LIMIT)


def _sigmoid(x):
    return 1.0 / (1.0 + jnp.exp(-x))


def _silu(x):
    return x * _sigmoid(x)


def _proj_kernel(x_ref, w_ref, o_ref):
    o_ref[...] = jnp.dot(x_ref[...].astype(BF16), w_ref[...], preferred_element_type=F32)


def _proj(x, w, tm):
    m, k = x.shape
    n = w.shape[1]
    return pl.pallas_call(
        _proj_kernel,
        grid=(m // tm,),
        in_specs=[pl.BlockSpec((tm, k), lambda i: (i, 0)), pl.BlockSpec((k, n), lambda i: (0, 0))],
        out_specs=pl.BlockSpec((tm, n), lambda i: (i, 0)),
        out_shape=jax.ShapeDtypeStruct((m, n), F32),
        compiler_params=_cparams(1),
        name="proj",
    )(x, w)


def _outproj_ln_kernel(y_ref, z_ref, x_ref, w_ref, g_ref, b_ref, o_ref):
    a = (y_ref[...] * _silu(z_ref[...])).astype(BF16)
    h = DEEPNORM_ALPHA * x_ref[...] + jnp.dot(a, w_ref[...], preferred_element_type=F32)
    mu = jnp.mean(h, axis=-1, keepdims=True)
    c = h - mu
    var = jnp.mean(c * c, axis=-1, keepdims=True)
    o_ref[...] = c * lax.rsqrt(var + LN_EPS) * g_ref[...] + b_ref[...]


def _outproj_ln(y, z_arr, z_col, x, w, g, b, tm):
    m = x.shape[0]
    row = lambda i: (i, 0)
    const = lambda i: (0, 0)
    return pl.pallas_call(
        _outproj_ln_kernel,
        grid=(m // tm,),
        in_specs=[
            pl.BlockSpec((tm, D_MODEL), row),
            pl.BlockSpec((tm, D_MODEL), lambda i: (i, z_col)),
            pl.BlockSpec((tm, D_MODEL), row),
            pl.BlockSpec((D_MODEL, D_MODEL), const),
            pl.BlockSpec((1, D_MODEL), const),
            pl.BlockSpec((1, D_MODEL), const),
        ],
        out_specs=pl.BlockSpec((tm, D_MODEL), row),
        out_shape=jax.ShapeDtypeStruct((m, D_MODEL), F32),
        compiler_params=_cparams(1),
        name="outproj_ln",
    )(y, z_arr, x, w, g.reshape(1, D_MODEL), b.reshape(1, D_MODEL))


def _rwkv_prep_kernel(p_ref, prev_ref, mu_ref, w0_ref, a0_ref, lora_ref, kk_ref, ka_ref,
                      r_o, w_o, k_o, v_o, kk_o, a_o, z_o, carry, *, tt, nb, log_decay):
    p = p_ref[...]
    row = lax.broadcasted_iota(jnp.int32, (tt, 1), 0)
    if nb == 1:
        @pl.when(pl.program_id(1) == 0)
        def _():
            carry[0:1, :] = prev_ref[0]

        before = carry[0:1, :]
        first = row == 0
    else:
        seq = tt // nb
        before = jnp.concatenate([jnp.broadcast_to(prev_ref[i], (seq, A_NCOLS)) for i in range(nb)], axis=0)
        first = (row & (seq - 1)) == 0
    p_shift = jnp.where(first, before, pltpu.roll(p, 1, axis=0))
    carry[0:1, :] = p[tt - 1:tt, :]
    pm = p + (p_shift - p) * mu_ref[...]
    r = pm[:, 0:A_WIDTH]
    k = pm[:, A_WIDTH:2 * A_WIDTH]
    v = pm[:, 2 * A_WIDTH:3 * A_WIDTH]
    z = pm[:, 3 * A_WIDTH:4 * A_WIDTH]
    lo = pm[:, 4 * A_WIDTH:A_NCOLS]
    is_w = lax.broadcasted_iota(jnp.int32, (1, 2 * A_LORA), 1) < A_LORA
    lo = jnp.where(is_w, jnp.tanh(lo), lo)
    lora = jnp.dot(lo.astype(BF16), lora_ref[...], preferred_element_type=F32)
    u = w0_ref[...] + lora[:, 0:A_WIDTH]
    w_log = jnp.minimum(u, 0.0) - jnp.log(1.0 + jnp.exp(-jnp.abs(u))) - 0.5
    a = _sigmoid(a0_ref[...] + lora[:, A_WIDTH:2 * A_WIDTH])
    r_o[...] = r
    w_o[...] = -jnp.exp(w_log) if log_decay else jnp.exp(-jnp.exp(w_log))
    k_o[...] = k * (1.0 + (a - 1.0) * ka_ref[...])
    v_o[...] = v
    kk_o[...] = k * kk_ref[...]
    a_o[...] = a
    z_o[...] = z


def _rwkv_prep(p, p_prev, mu, w0, a0, lora_w, k_k, k_a, batch, seq, tt, log_decay):
    nb = max(1, tt // seq)
    assert (seq % tt == 0 and nb == 1) or (tt % seq == 0 and batch % nb == 0 and seq & (seq - 1) == 0)
    nt = max(1, seq // tt)
    row = lambda b, t: (b * nt + t, 0)
    const = lambda b, t: (0, 0)
    vec = lambda a: a.reshape(1, -1)
    out = jax.ShapeDtypeStruct((batch * seq, A_WIDTH), F32)
    return pl.pallas_call(
        functools.partial(_rwkv_prep_kernel, tt=tt, nb=nb, log_decay=log_decay),
        grid=(batch // nb, nt),
        in_specs=[
            pl.BlockSpec((tt, A_NCOLS), row),
            pl.BlockSpec((nb, 1, A_NCOLS), lambda b, t: (b, 0, 0)),
            pl.BlockSpec((1, A_NCOLS), const),
            pl.BlockSpec((1, A_WIDTH), const),
            pl.BlockSpec((1, A_WIDTH), const),
            pl.BlockSpec((2 * A_LORA, 2 * A_WIDTH), const),
            pl.BlockSpec((1, A_WIDTH), const),
            pl.BlockSpec((1, A_WIDTH), const),
        ],
        out_specs=[pl.BlockSpec((tt, A_WIDTH), row)] * 7,
        out_shape=[out] * 7,
        scratch_shapes=[pltpu.VMEM((8, A_NCOLS), F32)],
        compiler_params=_cparams(2),
        name="rwkv_prep",
    )(p, p_prev.reshape(batch, 1, A_NCOLS), vec(mu), vec(w0), vec(a0), lora_w, vec(k_k), vec(k_a))


def _rwkv_scan_kernel(r_ref, w_ref, k_ref, v_ref, kkr_ref, a_ref, s0_ref, rk_ref, lg_ref, lb_ref,
                      y_ref, sout_ref, state, kk_s, b_s, wr_s, *, tt):
    t = pl.program_id(1)
    n = A_HEAD_DIM

    @pl.when(t == 0)
    def _():
        state[...] = s0_ref[...]

    kkr = kkr_ref[...]
    norm2 = jnp.sum(kkr * kkr, axis=1, keepdims=True)
    kk = kkr * lax.rsqrt(jnp.maximum(norm2, 1e-24))
    kk_s[...] = kk
    b_s[...] = kk * a_ref[...]
    wr_s[...] = w_ref[...] * r_ref[...]
    rk = rk_ref[...]
    lg = lg_ref[...]
    lb = lb_ref[...]

    def step(s, carry):
        sa = jnp.zeros((n, LANES), F32)
        y = jnp.zeros((n, LANES), F32)
        for j in range(n):
            sj = state[j]
            sa = sa - sj * kk_s[s, j:j + 1, :]
            y = y + sj * wr_s[s, j:j + 1, :]
        v = v_ref[s]
        for j in range(n):
            state[j] = state[j] * w_ref[s, j:j + 1, :] + sa * b_s[s, j:j + 1, :] + v * k_ref[s, j:j + 1, :]
        r = r_ref[s]
        k = k_ref[s]
        b_r = jnp.sum(b_s[s] * r, axis=0, keepdims=True)
        k_r = jnp.sum(k * r, axis=0, keepdims=True)
        bonus = jnp.sum(k * r * rk, axis=0, keepdims=True)
        y = y + sa * b_r + v * k_r
        mean = jnp.mean(y, axis=0, keepdims=True)
        c = y - mean
        var = jnp.mean(c * c, axis=0, keepdims=True)
        y_ref[s] = c * lax.rsqrt(var + A_GN_EPS) * lg + lb + bonus * v
        return carry

    lax.fori_loop(0, tt, step, 0)

    @pl.when(t == pl.num_programs(1) - 1)
    def _():
        sout_ref[...] = state[...]


def _rwkv_scan(r, w, k, v, kkr, a, s0, rk, lg, lb, tt):
    seq, n, chains = r.shape
    seq_blk = pl.BlockSpec((tt, n, LANES), lambda c, t: (t, 0, c))
    st_blk = pl.BlockSpec((n, n, LANES), lambda c, t: (0, 0, c))
    par_blk = pl.BlockSpec((n, LANES), lambda c, t: (0, c))
    return pl.pallas_call(
        functools.partial(_rwkv_scan_kernel, tt=tt),
        grid=(chains // LANES, seq // tt),
        in_specs=[seq_blk] * 6 + [st_blk] + [par_blk] * 3,
        out_specs=[seq_blk, st_blk],
        out_shape=[jax.ShapeDtypeStruct((seq, n, chains), F32), jax.ShapeDtypeStruct((n, n, chains), F32)],
        scratch_shapes=[pltpu.VMEM((n, n, LANES), F32)] + [pltpu.VMEM((tt, n, LANES), F32)] * 3,
        compiler_params=_cparams(2),
        name="rwkv_scan",
    )(r, w, k, v, kkr, a, s0, rk, lg, lb)


def _split3(x):
    hi = x.astype(BF16)
    rest = x - hi.astype(F32)
    mid = rest.astype(BF16)
    return hi, mid, (rest - mid.astype(F32)).astype(BF16)


def _dot_sel(x, sel):
    hi, mid, lo = _split3(x)
    d = lambda a: jnp.dot(a, sel, preferred_element_type=F32)
    return d(hi) + (d(mid) + d(lo))


def _rwkv_chunk_kernel(r_ref, lw_ref, k_ref, v_ref, kkr_ref, a_ref, s0_ref, rk_ref, lg_ref, lb_ref,
                       y_ref, sout_ref, state, *, ln, bb):
    c = pl.program_id(1)
    half = A_HEAD_DIM
    n_pairs = A_WIDTH // LANES

    @pl.when(c == 0)
    def _():
        state[...] = s0_ref[...]

    row = lax.broadcasted_iota(jnp.int32, (ln, 1), 0)
    lo = lax.broadcasted_iota(jnp.int32, (1, LANES), 1) < half
    r2 = lax.broadcasted_iota(jnp.int32, (2 * ln, ln), 0) & (ln - 1)
    c2 = lax.broadcasted_iota(jnp.int32, (2 * ln, ln), 1)
    strict2, incl2 = c2 < r2, c2 <= r2
    rr = lax.broadcasted_iota(jnp.int32, (LANES, LANES), 0)
    cc = lax.broadcasted_iota(jnp.int32, (LANES, LANES), 1)
    same_head = (rr & half) == (cc & half)
    seg = jnp.where(same_head, 1.0, 0.0).astype(BF16)
    rb = lax.broadcasted_iota(jnp.int32, (2 * ln, 2 * ln), 0)
    cb = lax.broadcasted_iota(jnp.int32, (2 * ln, 2 * ln), 1)
    strict_bd = ((cb & (ln - 1)) + jnp.where((rb & ln) == (cb & ln), 0, ln)) < (rb & (ln - 1))
    split = lambda x: jnp.concatenate([jnp.where(lo, x, 0.0), jnp.where(lo, 0.0, x)], axis=0)
    merge = lambda x2: jnp.where(lo, x2[0:ln], x2[ln:2 * ln])
    mm = lambda a, b: jnp.dot(a.astype(BF16), b.astype(BF16), preferred_element_type=F32)

    pairs = range(bb * n_pairs)
    sls = [slice((p % n_pairs) * LANES, (p % n_pairs + 1) * LANES) for p in pairs]
    rws = [slice((p // n_pairs) * ln, (p // n_pairs + 1) * ln) for p in pairs]
    load = lambda ref: [ref[rw, sl] for rw, sl in zip(rws, sls)]
    r, lw, k, v, kkr, a = (load(ref) for ref in (r_ref, lw_ref, k_ref, v_ref, kkr_ref, a_ref))
    norm2 = [_dot_sel(x * x, seg) for x in kkr]
    kk = [x * lax.rsqrt(jnp.maximum(n2, 1e-24)) for x, n2 in zip(kkr, norm2)]
    bv = [x * y for x, y in zip(kk, a)]
    cum = lw
    d = 1
    while d < ln:
        cum = [x + jnp.where(row >= d, pltpu.roll(x, d, axis=0), 0.0) for x in cum]
        d *= 2
    tot = [x[ln - 1:ln, :] for x in cum]
    kkd = [x * jnp.exp(cm - l) for x, cm, l in zip(kk, cum, lw)]
    rd = [x * jnp.exp(cm) for x, cm in zip(r, cum)]
    w_inv = [jnp.exp(-cm) for cm in cum]
    kd = [x * w for x, w in zip(k, w_inv)]
    bd = [x * w for x, w in zip(bv, w_inv)]
    s_old = [state[p // n_pairs, p % n_pairs] for p in pairs]

    xk = [split(x).astype(BF16) for x in kkd]
    xr = [split(x).astype(BF16) for x in rd]
    nil = [jnp.where(strict_bd, -_dot_t(x, split(y).astype(BF16)), 0.0) for x, y in zip(xk, bd)]
    gk = [_dot_t(jnp.concatenate([x, y], axis=0), z.astype(BF16)) for x, y, z in zip(xk, xr, kd)]
    sx = [_dot_t(jnp.concatenate([x, y], axis=0).astype(BF16), s.astype(BF16)) for x, y, s in zip(kkd, rd, s_old)]
    av = [mm(jnp.concatenate([jnp.where(strict2, g[0:2 * ln], 0.0), jnp.where(incl2, g[2 * ln:4 * ln], 0.0)], axis=0), x)
          for g, x in zip(gk, v)]
    u2 = [split(s[0:ln] + merge(x[0:2 * ln])) for s, x in zip(sx, av)]
    power = nil
    step = 1
    while step < ln:
        u2 = [x + mm(pw, x) for x, pw in zip(u2, power)]
        step *= 2
        if step < ln:
            power = [mm(pw, pw) for pw in power]
    u = [x[0:ln] + x[ln:2 * ln] for x in u2]
    a_rb = [jnp.where(incl2, _dot_t(x, y.astype(BF16)), 0.0) for x, y in zip(xr, bd)]
    y = [s[ln:2 * ln] + merge(x[2 * ln:4 * ln]) - merge(mm(g, w)) for s, x, g, w in zip(sx, av, a_rb, u)]

    inv_n = 1.0 / half
    for p in pairs:
        sl = sls[p]
        w_rest = jnp.exp(tot[p] - cum[p])
        upd = mm(jnp.concatenate([v[p], u[p]], axis=0).T,
                 jnp.concatenate([k[p] * w_rest, -(bv[p] * w_rest)], axis=0))
        state[p // n_pairs, p % n_pairs] = s_old[p] * jnp.exp(tot[p]) + jnp.where(same_head, upd, 0.0)
        mean = _dot_sel(y[p], seg) * inv_n
        cen = y[p] - mean
        var = _dot_sel(cen * cen, seg) * inv_n
        bonus = _dot_sel(r[p] * k[p] * rk_ref[:, sl], seg)
        y_ref[rws[p], sl] = cen * lax.rsqrt(var + A_GN_EPS) * lg_ref[:, sl] + lb_ref[:, sl] + bonus * v[p]

    @pl.when(c == pl.num_programs(1) - 1)
    def _():
        sout_ref[...] = state[...]


def _rwkv_chunked(r, lw, k, v, kkr, a, s0, r_k, lnx_g, lnx_b, batch, seq, ln):
    nc = seq // ln
    n_pairs = A_WIDTH // LANES
    bb = RWKV_ROWS_PER_STEP if nc == 1 and batch % RWKV_ROWS_PER_STEP == 0 else 1
    row = pl.BlockSpec((bb * ln, A_WIDTH), lambda b, c: (b * nc + c, 0))
    st = pl.BlockSpec((bb, n_pairs, LANES, LANES), lambda b, c: (b, 0, 0, 0))
    par = pl.BlockSpec((1, A_WIDTH), lambda b, c: (0, 0))
    return pl.pallas_call(
        functools.partial(_rwkv_chunk_kernel, ln=ln, bb=bb),
        grid=(batch // bb, nc),
        in_specs=[row] * 6 + [st] + [par] * 3,
        out_specs=[row, st],
        out_shape=[jax.ShapeDtypeStruct((batch * seq, A_WIDTH), F32),
                   jax.ShapeDtypeStruct((batch, n_pairs, LANES, LANES), F32)],
        scratch_shapes=[pltpu.VMEM((bb, n_pairs, LANES, LANES), F32)],
        compiler_params=_cparams(2),
        name="rwkv_chunk",
    )(r, lw, k, v, kkr, a, s0, r_k.reshape(1, A_WIDTH), lnx_g.reshape(1, A_WIDTH), lnx_b.reshape(1, A_WIDTH))


def _rwkv_layer(x, batch, seq, s0, p_prev, prm, ln_g, ln_b, tm, tt_prep, tt_scan):
    (w_in, mu, w0, w2, a0, a2, k_k, k_a, r_k, lnx_g, lnx_b, w_out) = prm
    h, n = A_HEADS, A_HEAD_DIM
    ln = min(RWKV_CHUNK, seq)
    chunked = seq % ln == 0 and ln % 8 == 0 and ln & (ln - 1) == 0
    p = _proj(x, w_in.astype(BF16), tm)
    zeros = jnp.zeros((A_LORA, A_WIDTH), F32)
    lora_w = jnp.concatenate([jnp.concatenate([w2, zeros], axis=1), jnp.concatenate([zeros, a2], axis=1)], axis=0)
    r, w, k, v, kkr, a, z = _rwkv_prep(p, p_prev, mu, w0, a0, lora_w.astype(BF16), k_k, k_a, batch, seq, tt_prep,
                                       chunked)
    p_last = p.reshape(batch, seq, A_NCOLS)[:, -1]
    if chunked:
        eye2 = jnp.eye(2, dtype=F32)
        s0_bd = (s0.reshape(batch, h // 2, 2, n, 1, n) * eye2[None, None, :, None, :, None]).reshape(
            batch, h // 2, 2 * n, 2 * n)
        y, s_bd = _rwkv_chunked(r, w, k, v, kkr, a, s0_bd, r_k, lnx_g, lnx_b, batch, seq, ln)
        s_bd = s_bd.reshape(batch, h // 2, 2, n, 2, n)
        s_final = jnp.stack([s_bd[:, :, 0, :, 0, :], s_bd[:, :, 1, :, 1, :]], axis=2).reshape(batch, h, n, n)
        x_new = _outproj_ln(y, z, 0, x, w_out.astype(BF16), ln_g, ln_b, tm)
        return x_new, s_final, p_last
    chains = batch * h
    to_scan = lambda t: t.reshape(batch, seq, h, n).transpose(1, 3, 0, 2).reshape(seq, n, chains)
    per_chain = lambda t: jnp.broadcast_to(t.reshape(h, n).T[:, None, :], (n, batch, h)).reshape(n, chains)
    s0_t = s0.transpose(3, 2, 0, 1).reshape(n, n, chains)
    y, s_out = _rwkv_scan(to_scan(r), to_scan(w), to_scan(k), to_scan(v), to_scan(kkr), to_scan(a), s0_t,
                          per_chain(r_k), per_chain(lnx_g), per_chain(lnx_b), tt_scan)
    y = y.reshape(seq, n, batch, h).transpose(2, 0, 3, 1).reshape(batch * seq, A_WIDTH)
    x_new = _outproj_ln(y, z, 0, x, w_out.astype(BF16), ln_g, ln_b, tm)
    s_final = s_out.reshape(n, n, batch, h).transpose(2, 3, 1, 0)
    return x_new, s_final, p_last


def _pool_kernel(u_ref, buf_ref, wg_ref, scale_ref, y_ref, nbuf_ref, ext, *, tt, front_valid):
    t = pl.program_id(1)

    @pl.when(t == 0)
    def _():
        ext[1:B_CARRY, :] = buf_ref[0]

    @pl.when(t > 0)
    def _():
        ext[0:B_CARRY, :] = ext[tt:tt + B_CARRY, :]

    u = u_ref[...]
    ext[B_CARRY:B_CARRY + tt, :] = u
    pos = t * tt + lax.broadcasted_iota(jnp.int32, (tt, 1), 0)
    for gi, win in enumerate(B_WINDOWS):
        lo, hi = gi * B_GROUP_W, (gi + 1) * B_GROUP_W
        ug = u[:, lo:hi]
        acc = ug
        for s in range(1, win):
            acc = acc + ext[B_CARRY - s:B_CARRY - s + tt, lo:hi]
        if front_valid:
            cnt = float(win)
        else:
            cnt = jnp.minimum(pos + 1, win).astype(F32)
        d = acc / cnt - ug
        yg = jnp.dot(d.astype(BF16), wg_ref[gi], preferred_element_type=F32)
        y_ref[:, lo:hi] = yg * scale_ref[:, lo:hi]

    @pl.when(t == pl.num_programs(1) - 1)
    def _():
        nbuf_ref[0] = ext[tt + 1:tt + B_CARRY, :]


def _pool(uz, buf, w_grp, scale, batch, seq, tt, front_valid):
    nt = seq // tt
    return pl.pallas_call(
        functools.partial(_pool_kernel, tt=tt, front_valid=front_valid),
        grid=(batch, nt),
        in_specs=[
            pl.BlockSpec((tt, B_WIDTH), lambda b, t: (b * nt + t, 0)),
            pl.BlockSpec((1, B_BUF, B_WIDTH), lambda b, t: (b, 0, 0)),
            pl.BlockSpec((len(B_WINDOWS), B_GROUP_W, B_GROUP_W), lambda b, t: (0, 0, 0)),
            pl.BlockSpec((1, B_WIDTH), lambda b, t: (0, 0)),
        ],
        out_specs=[
            pl.BlockSpec((tt, B_WIDTH), lambda b, t: (b * nt + t, 0)),
            pl.BlockSpec((1, B_BUF, B_WIDTH), lambda b, t: (b, 0, 0)),
        ],
        out_shape=[jax.ShapeDtypeStruct((batch * seq, B_WIDTH), F32),
                   jax.ShapeDtypeStruct((batch, B_BUF, B_WIDTH), F32)],
        scratch_shapes=[pltpu.VMEM((tt + B_CARRY, B_WIDTH), F32)],
        compiler_params=_cparams(2),
        name="pool",
    )(uz, buf, w_grp, scale.reshape(1, B_WIDTH))


def _pool_layer(x, batch, seq, buf, front_valid, prm, ln_g, ln_b, tm, tt):
    w_in, w_grp, scale, w_out = prm
    uz = _proj(x, w_in.astype(BF16), tm)
    y, new_buf = _pool(uz, buf, w_grp.astype(BF16), scale, batch, seq, tt, front_valid)
    x_new = _outproj_ln(y, uz, 1, x, w_out.astype(BF16), ln_g, ln_b, tm)
    return x_new, new_buf


NEG = -1e30
C_SCALE = C_HEAD_DIM ** -0.5


def _alibi_slopes():
    return jnp.power(2.0, -8.0 * (jnp.arange(C_HEADS, dtype=F32) + 1.0) / C_HEADS)


def _dot_t(a, b):
    return lax.dot_general(a, b, (((1,), (1,)), ((), ())), preferred_element_type=F32)


def _masked_softmax_parts(scores, masks):
    m = None
    for s, k in zip(scores, masks):
        part = jnp.max(jnp.where(k, s, -jnp.inf), axis=1, keepdims=True)
        m = part if m is None else jnp.maximum(m, part)
    m = jnp.where(jnp.isfinite(m), m, 0.0)
    es = [jnp.where(k, jnp.exp(s - m), 0.0) for s, k in zip(scores, masks)]
    den = None
    for e in es:
        part = jnp.sum(e, axis=1, keepdims=True)
        den = part if den is None else den + part
    den = jnp.maximum(den, 1e-30)
    return [e / den for e in es]


def _flash_init(rows, width):
    return (jnp.full((rows, 1), NEG, F32), jnp.zeros((rows, 1), F32), jnp.zeros((rows, width), F32))


def _flash_update(carry, qb, kb, vb, bias):
    m, l, acc = carry
    s, mask = bias(_dot_t(qb, kb))
    s = jnp.where(mask, s, NEG)
    m_new = jnp.maximum(m, jnp.max(s, axis=1, keepdims=True))
    alpha = jnp.exp(m - m_new)
    p = jnp.where(mask, jnp.exp(s - m_new), 0.0)
    l = alpha * l + jnp.sum(p, axis=1, keepdims=True)
    acc = alpha * acc + jnp.dot(p.astype(BF16), vb, preferred_element_type=F32)
    return m_new, l, acc


def _flash_finish(carry):
    _, l, acc = carry
    return acc / jnp.maximum(l, 1e-30)


def _top_n_mask(imp, top_n):
    n = imp.shape[1]
    idx = lax.broadcasted_iota(jnp.int32, (1, n), 1)
    rank = jnp.zeros(imp.shape, F32)
    for c in range(n):
        col = imp[:, c:c + 1]
        rank = rank + jnp.where(idx > c, jnp.where(col >= imp, 1.0, 0.0), jnp.where(col > imp, 1.0, 0.0))
    return jnp.where(rank < top_n, jnp.where(imp >= 0.0, 1.0, 0.0), 0.0)


def _block_expand(n_blocks, kpos):
    blk = lax.broadcasted_iota(jnp.int32, (n_blocks, kpos.shape[1]), 0)
    return jnp.where(lax.shift_right_logical(kpos, 6) == blk, 1.0, 0.0).astype(BF16)


def _cmp_kernel(kv_ref, w_ref, e_ref, o_ref):
    x = kv_ref[...]
    n_pair = x.shape[0] // (2 * C_CMP_BLOCK)
    x4 = x.reshape(n_pair, 2, C_CMP_BLOCK, x.shape[1])
    w = w_ref[...][None]
    e_ref[0] = jnp.sum(x4[:, 0] * w, axis=1)
    o_ref[0] = jnp.sum(x4[:, 1] * w, axis=1)


def _cmp(p, col_block, w, batch, seq):
    width = w.shape[1]
    n_pair = seq // (2 * C_CMP_BLOCK)
    out = jax.ShapeDtypeStruct((batch, n_pair, width), F32)
    return pl.pallas_call(
        _cmp_kernel,
        grid=(batch,),
        in_specs=[pl.BlockSpec((seq, width), lambda b: (b, col_block)), pl.BlockSpec(w.shape, lambda b: (0, 0))],
        out_specs=[pl.BlockSpec((1, n_pair, width), lambda b: (b, 0, 0))] * 2,
        out_shape=[out, out],
        compiler_params=_cparams(1),
        name="nsa_cmp",
    )(p, w)


def _nsa_prompt_kernel(q_ref, g_ref, ce_ref, co_ref, sel_ref, win_ref, slope_ref, o_ref, *, tq, tk):
    q0 = pl.program_id(2) * tq
    hd = C_HEAD_DIM
    rows = C_HPG * tq
    n_pair = ce_ref.shape[1]
    stack = lambda f: jnp.concatenate([f(j) for j in range(C_HPG)], axis=0)
    q_all = q_ref[...]
    qb = (stack(lambda j: q_all[:, j * hd:(j + 1) * hd]) * C_SCALE).astype(BF16)
    slope = stack(lambda j: jnp.broadcast_to(slope_ref[0, j:j + 1, 0:1], (tq, 1)))
    qpos_t = q0 + lax.broadcasted_iota(jnp.int32, (tq, 1), 0)
    qpos = stack(lambda j: qpos_t)
    gates = _sigmoid(g_ref[...])
    gate = lambda br: stack(lambda j: gates[:, 3 * j + br:3 * j + br + 1])

    pair = lax.broadcasted_iota(jnp.int32, (1, n_pair), 1)
    ce, co = ce_ref[0], co_ref[0]
    scores, masks = [], []
    for par, c in ((0, ce), (1, co)):
        dist = qpos - ((2 * pair + par + 1) * C_CMP_BLOCK - 1)
        scores.append(_dot_t(qb, c[:, :hd].astype(BF16)) - slope * dist.astype(F32))
        masks.append(dist >= 0)
    p_e, p_o = _masked_softmax_parts(scores, masks)
    o_c = (jnp.dot(p_e.astype(BF16), ce[:, hd:].astype(BF16), preferred_element_type=F32)
           + jnp.dot(p_o.astype(BF16), co[:, hd:].astype(BF16), preferred_element_type=F32))
    imp_h = p_e + p_o
    imp = imp_h[0:tq]
    for j in range(1, C_HPG):
        imp = imp + imp_h[j * tq:(j + 1) * tq]
    cur = lax.shift_right_logical(qpos_t, 6)
    imp = jnp.where(pair == cur, C_FORCE, imp)
    imp = jnp.where(pair <= cur, imp, -1.0)
    sel = _top_n_mask(imp, C_TOP_N)
    sel_rows = stack(lambda j: sel).astype(BF16)

    def sel_body(c, carry):
        k0 = pl.multiple_of(c * tk, tk)
        kv = sel_ref[pl.ds(k0, tk), :]
        kpos = k0 + lax.broadcasted_iota(jnp.int32, (1, tk), 1)
        chosen = jnp.dot(sel_rows, _block_expand(n_pair, kpos), preferred_element_type=F32)

        def bias(s):
            dist = qpos - kpos
            return s - slope * dist.astype(F32), jnp.where(dist >= 0, chosen, 0.0) > 0.5

        return _flash_update(carry, qb, kv[:, :hd].astype(BF16), kv[:, hd:].astype(BF16), bias)

    o_s = _flash_finish(lax.fori_loop(0, (q0 + tq) // tk, sel_body, _flash_init(rows, hd)))

    w0 = jnp.maximum(q0 - C_WINDOW, 0)

    def win_body(c, carry):
        k0 = pl.multiple_of(w0 + c * tk, tk)
        kv = win_ref[pl.ds(k0, tk), :]
        kpos = k0 + lax.broadcasted_iota(jnp.int32, (1, tk), 1)

        def bias(s):
            dist = qpos - kpos
            return s - slope * dist.astype(F32), jnp.where(dist >= 0, dist, C_WINDOW) < C_WINDOW

        return _flash_update(carry, qb, kv[:, :hd].astype(BF16), kv[:, hd:].astype(BF16), bias)

    o_w = _flash_finish(lax.fori_loop(0, (tq + C_WINDOW) // tk, win_body, _flash_init(rows, hd)))

    o = gate(0) * o_c + gate(1) * o_s + gate(2) * o_w
    o_ref[...] = jnp.concatenate([o[j * tq:(j + 1) * tq] for j in range(C_HPG)], axis=1)


def _nsa_prompt_layer(x, batch, seq, prm, ln_g, ln_b):
    w_in, cmp_wk, cmp_wv, w_out = prm
    hd, grp = C_HEAD_DIM, C_KV_HEADS
    tq = tk = 256
    assert seq % tq == 0 and seq >= tq + C_WINDOW and C_WINDOW % tk == 0 and seq % (2 * C_CMP_BLOCK) == 0
    kv_w = lambda i: w_in[:, C_WIDTH + C_KV_WIDTH * i:C_WIDTH + C_KV_WIDTH * (i + 1)].reshape(D_MODEL, grp, hd)
    kv_pair = lambda a, b: jnp.concatenate([kv_w(a), kv_w(b)], axis=2).reshape(D_MODEL, grp * 2 * hd)
    g_lo = C_WIDTH + 6 * C_KV_WIDTH
    g_w = w_in[:, g_lo:g_lo + 3 * C_HEADS].reshape(D_MODEL, grp, 3 * C_HPG)
    g_w = jnp.pad(g_w, ((0, 0), (0, 0), (0, LANES - 3 * C_HPG))).reshape(D_MODEL, grp * LANES)
    w_p = jnp.concatenate([w_in[:, :C_WIDTH], w_in[:, g_lo + 3 * C_HEADS:], kv_pair(0, 1), kv_pair(2, 3),
                           kv_pair(4, 5), g_w], axis=1)
    p = _proj(x, w_p.astype(BF16), 512)
    kv_lo = 2 * C_WIDTH
    kvw = grp * 2 * hd
    cw = jnp.concatenate([jnp.broadcast_to(cmp_wk[:, None], (C_CMP_BLOCK, hd)),
                          jnp.broadcast_to(cmp_wv[:, None], (C_CMP_BLOCK, hd))], axis=1)
    ce, co = _cmp(p, kv_lo // kvw, jnp.tile(cw, (1, grp)), batch, seq)
    slopes = jnp.broadcast_to(jnp.pad(_alibi_slopes().reshape(grp, C_HPG), ((0, 0), (0, 8 - C_HPG)))[:, :, None],
                              (grp, 8, LANES))
    nq = seq // tq
    n_pair = seq // (2 * C_CMP_BLOCK)
    gw = 2 * hd
    o = pl.pallas_call(
        functools.partial(_nsa_prompt_kernel, tq=tq, tk=tk),
        grid=(batch, grp, nq),
        in_specs=[
            pl.BlockSpec((tq, C_HPG * hd), lambda b, g, i: (b * nq + i, g)),
            pl.BlockSpec((tq, LANES), lambda b, g, i: (b * nq + i, (kv_lo + 3 * kvw) // LANES + g)),
            pl.BlockSpec((1, n_pair, gw), lambda b, g, i: (b, 0, g)),
            pl.BlockSpec((1, n_pair, gw), lambda b, g, i: (b, 0, g)),
            pl.BlockSpec((seq, gw), lambda b, g, i: (b, (kv_lo + kvw) // gw + g)),
            pl.BlockSpec((seq, gw), lambda b, g, i: (b, (kv_lo + 2 * kvw) // gw + g)),
            pl.BlockSpec((1, 8, LANES), lambda b, g, i: (g, 0, 0)),
        ],
        out_specs=pl.BlockSpec((tq, C_HPG * hd), lambda b, g, i: (b * nq + i, g)),
        out_shape=jax.ShapeDtypeStruct((batch * seq, C_WIDTH), F32),
        compiler_params=_cparams(3),
        name="nsa_prompt",
    )(p, p, ce, co, p, p, slopes)
    x_new = _outproj_ln(o, p, 1, x, w_out.astype(BF16), ln_g, ln_b, 512)
    branch = lambda i: p[:, kv_lo + i * kvw:kv_lo + (i + 1) * kvw].reshape(batch, seq, grp, 2, hd)
    rows = (branch(0)[:, :, :, 0], branch(0)[:, :, :, 1], branch(1)[:, :, :, 0], branch(1)[:, :, :, 1])
    keep = min(C_WINDOW, seq)
    return x_new, rows, branch(2)[:, seq - keep:, :, 0], branch(2)[:, seq - keep:, :, 1]


def _nsa_sample_kernel(tbl_ref, *refs, n_pages, ts):
    del tbl_ref
    ck, cv, sk, sv = (refs[i * n_pages:(i + 1) * n_pages] for i in range(4))
    (q_ref, g_ref, ksn_ref, vsn_ref, kwn_ref, vwn_ref, wk_ref, wv_ref, cwk_ref, cwv_ref, slope_ref,
     o_ref, ke_s, ko_s, ve_s, vo_s, new_s) = refs[4 * n_pages:]
    hd, grp = C_HEAD_DIM, C_KV_HEADS
    rows = C_HEADS * ts
    past = n_pages * PAGE_SIZE
    n_pair = past // C_SEL_BLOCK
    qpos = past + lax.rem(lax.broadcasted_iota(jnp.int32, (rows, 1), 0), ts)
    qb = (q_ref[0] * C_SCALE).astype(BF16)
    slope = slope_ref[:, 0:1]

    per_page = PAGE_SIZE // C_CMP_BLOCK
    for pages, cw_ref, e_s, o_s in ((ck, cwk_ref, ke_s, ko_s), (cv, cwv_ref, ve_s, vo_s)):
        cw = cw_ref[...][None]
        for pg in range(n_pages):
            tok = jnp.sum(pages[pg][0].reshape(per_page, C_CMP_BLOCK, grp * hd) * cw, axis=1)
            for i in range(per_page):
                n = pg * per_page + i
                dst = e_s if n % 2 == 0 else o_s
                dst[n // 2:n // 2 + 1, :] = tok[i:i + 1]

    pair = lax.broadcasted_iota(jnp.int32, (1, n_pair), 1)
    scores, masks = [], []
    for par, k_s in ((0, ke_s), (1, ko_s)):
        dist = qpos - ((2 * pair + par + 1) * C_CMP_BLOCK - 1)
        scores.append(_dot_t(qb, k_s[...].astype(BF16)) - slope * dist.astype(F32))
        masks.append(dist >= 0)
    p_e, p_o = _masked_softmax_parts(scores, masks)
    o_c = (jnp.dot(p_e.astype(BF16), ve_s[...].astype(BF16), preferred_element_type=F32)
           + jnp.dot(p_o.astype(BF16), vo_s[...].astype(BF16), preferred_element_type=F32))
    imp_h = p_e + p_o
    sel_rows = []
    for g in range(grp):
        base = g * C_HPG * ts
        imp = imp_h[base:base + ts]
        for j in range(1, C_HPG):
            imp = imp + imp_h[base + j * ts:base + (j + 1) * ts]
        sel_g = _top_n_mask(imp, C_TOP_N - 1)
        sel_rows += [sel_g] * C_HPG
    sel_rows = jnp.concatenate(sel_rows, axis=0).astype(BF16)

    lane_pos = lax.broadcasted_iota(jnp.int32, (1, PAGE_SIZE), 1)

    def new_rows(k_ref, v_ref):
        new_s[...] = jnp.zeros(new_s.shape, F32)
        new_s[0, 0:ts, :] = k_ref[...]
        new_s[1, 0:ts, :] = v_ref[...]
        return new_s[0].astype(BF16), new_s[1].astype(BF16)

    def causal(kpos):
        def bias(s):
            dist = qpos - kpos
            return s - slope * dist.astype(F32), dist >= 0
        return bias

    def window(kpos):
        def bias(s):
            dist = qpos - kpos
            return s - slope * dist.astype(F32), jnp.where(dist >= 0, dist, C_WINDOW) < C_WINDOW
        return bias

    carry = _flash_init(rows, grp * hd)
    for pg in range(n_pages):
        kpos = pg * PAGE_SIZE + lane_pos
        chosen = jnp.dot(sel_rows, _block_expand(n_pair, kpos), preferred_element_type=F32)

        def bias(s, kpos=kpos, chosen=chosen):
            dist = qpos - kpos
            return s - slope * dist.astype(F32), jnp.where(dist >= 0, chosen, 0.0) > 0.5

        carry = _flash_update(carry, qb, sk[pg][0].astype(BF16), sv[pg][0].astype(BF16), bias)
    kb, vb = new_rows(ksn_ref, vsn_ref)
    o_s = _flash_finish(_flash_update(carry, qb, kb, vb, causal(past + lane_pos)))

    carry = _flash_init(rows, grp * hd)
    n_win = wk_ref.shape[1]
    for c in range(n_win // PAGE_SIZE):
        lo = c * PAGE_SIZE
        carry = _flash_update(carry, qb, wk_ref[0, lo:lo + PAGE_SIZE, :].astype(BF16),
                              wv_ref[0, lo:lo + PAGE_SIZE, :].astype(BF16), window(past - n_win + lo + lane_pos))
    kb, vb = new_rows(kwn_ref, vwn_ref)
    o_w = _flash_finish(_flash_update(carry, qb, kb, vb, window(past + lane_pos)))

    per_group = C_HPG * ts
    diag = lambda full: jnp.concatenate(
        [full[g * per_group:(g + 1) * per_group, g * hd:(g + 1) * hd] for g in range(grp)], axis=0)
    gates = _sigmoid(g_ref[0])
    o_ref[0] = gates[:, 0:1] * diag(o_c) + gates[:, 1:2] * diag(o_s) + gates[:, 2:3] * diag(o_w)


def _nsa_sample_layer(x, batch, ts, caches, page_table, win_k, win_v, prm, ln_g, ln_b):
    w_in, cmp_wk, cmp_wv, w_out = prm
    hd, grp = C_HEAD_DIM, C_KV_HEADS
    n_pages = page_table.shape[1]
    n_win = win_k.shape[1]
    past = n_pages * PAGE_SIZE
    rows = C_HEADS * ts
    assert past % C_SEL_BLOCK == 0 and ts <= C_SEL_BLOCK and ts % 8 == 0 and n_win == C_WINDOW and past >= n_win
    g_lo = C_WIDTH + 6 * C_KV_WIDTH
    w_s = jnp.concatenate([w_in[:, :C_WIDTH], w_in[:, g_lo + 3 * C_HEADS:], w_in[:, C_WIDTH:g_lo],
                           jnp.pad(w_in[:, g_lo:g_lo + 3 * C_HEADS], ((0, 0), (0, LANES - 3 * C_HEADS)))], axis=1)
    p = _proj(x, w_s.astype(BF16), 512)
    kv_lo = 2 * C_WIDTH
    q = p[:, :C_WIDTH].reshape(batch, ts, grp, C_HPG, hd).transpose(0, 2, 3, 1, 4).reshape(batch, grp, C_HPG * ts, hd)
    q_bd = (q[:, :, :, None, :] * jnp.eye(grp, dtype=F32)[None, :, None, :, None]).reshape(batch, rows, grp * hd)
    g_lo_p = kv_lo + 6 * C_KV_WIDTH
    g_t = p[:, g_lo_p:g_lo_p + 3 * C_HEADS].reshape(batch, ts, C_HEADS, 3).transpose(0, 2, 1, 3).reshape(batch, rows, 3)
    slope_rows = jnp.broadcast_to(jnp.repeat(_alibi_slopes(), ts)[:, None], (rows, LANES))
    bcast = lambda w: jnp.broadcast_to(w[:, None], (C_CMP_BLOCK, grp * hd))
    pools = [c.reshape(c.shape[0], PAGE_SIZE, grp * hd) for c in caches]
    page_spec = lambda pg: pl.BlockSpec((1, PAGE_SIZE, grp * hd), lambda b, tbl: (tbl[b, pg], 0, 0))
    new_spec = lambda i: pl.BlockSpec((ts, C_KV_WIDTH), lambda b, tbl: (b, kv_lo // C_KV_WIDTH + i))
    const2 = lambda shape: pl.BlockSpec(shape, lambda b, tbl: (0, 0))
    per_b = lambda shape: pl.BlockSpec((1,) + shape, lambda b, tbl: (b, 0, 0))
    in_specs = ([page_spec(pg) for _ in range(4) for pg in range(n_pages)]
                + [per_b((rows, grp * hd)), per_b((rows, 3)), new_spec(2), new_spec(3), new_spec(4), new_spec(5),
                   per_b((n_win, grp * hd)), per_b((n_win, grp * hd)),
                   const2((C_CMP_BLOCK, grp * hd)), const2((C_CMP_BLOCK, grp * hd)), const2((rows, LANES))])
    n_pair = past // C_SEL_BLOCK
    o = pl.pallas_call(
        functools.partial(_nsa_sample_kernel, n_pages=n_pages, ts=ts),
        grid_spec=pltpu.PrefetchScalarGridSpec(
            num_scalar_prefetch=1,
            grid=(batch,),
            in_specs=in_specs,
            out_specs=per_b((rows, hd)),
            scratch_shapes=[pltpu.VMEM((n_pair, grp * hd), F32)] * 4 + [pltpu.VMEM((2, PAGE_SIZE, grp * hd), F32)],
        ),
        out_shape=jax.ShapeDtypeStruct((batch, rows, hd), F32),
        compiler_params=_cparams(1),
        name="nsa_sample",
    )(page_table, *[pool for pool in pools for _ in range(n_pages)], q_bd, g_t, p, p, p, p,
      win_k.reshape(batch, n_win, grp * hd), win_v.reshape(batch, n_win, grp * hd), bcast(cmp_wk), bcast(cmp_wv),
      slope_rows)
    y = o.reshape(batch, C_HEADS, ts, hd).transpose(0, 2, 1, 3).reshape(batch * ts, C_WIDTH)
    x_new = _outproj_ln(y, p, 1, x, w_out.astype(BF16), ln_g, ln_b, 512)
    new = lambda i: p[:, kv_lo + i * C_KV_WIDTH:kv_lo + (i + 1) * C_KV_WIDTH].reshape(batch, ts, grp, hd)
    keep = lambda buf, i: jnp.concatenate([buf, new(i)], axis=1)[:, -n_win:]
    return x_new, (new(0), new(1), new(2), new(3)), keep(win_k, 4), keep(win_v, 5)


def kernel(x_prompt, x_sample, state_rwkv_S, state_rwkv_shift, state_pool, cache_cmp_k, cache_cmp_v, cache_sel_k,
           cache_sel_v, state_win_k, state_win_v, page_table, ln_g, ln_b, a_w_in, a_mu, a_w0, a_w2, a_a0, a_a2, a_k_k,
           a_k_a, a_r_k, a_lnx_g, a_lnx_b, a_w_out, b_w_in, b_w_grp, b_scale, b_w_out, c_w_in, c_cmp_wk, c_cmp_wv,
           c_w_out):
    bp, tp, _ = x_prompt.shape
    bs, ts, _ = x_sample.shape
    xp = x_prompt.reshape(bp * tp, D_MODEL)
    xs = x_sample.reshape(bs * ts, D_MODEL)
    s_p, s_s, sh_p, sh_s, pl_p, pl_s = [], [], [], [], [], []
    rows_p, rows_s, wk_p, wk_s, wv_p, wv_s = [], [], [], [], [], []
    for layer in range(DEPTH):
        kind, li = layer % 3, layer // 3
        g, b = ln_g[layer], ln_b[layer]
        if kind == 0:
            prm = (a_w_in[li], a_mu[li], a_w0[li], a_w2[li], a_a0[li], a_a2[li], a_k_k[li], a_k_a[li], a_r_k[li],
                   a_lnx_g[li], a_lnx_b[li], a_w_out[li])
            xp, s_new, sh_new = _rwkv_layer(xp, bp, tp, jnp.zeros((bp, A_HEADS, A_HEAD_DIM, A_HEAD_DIM), F32),
                                            jnp.zeros((bp, A_NCOLS), F32), prm, g, b, 512, 256, 32)
            s_p.append(s_new)
            sh_p.append(sh_new)
            xs, s_new, sh_new = _rwkv_layer(xs, bs, ts, state_rwkv_S[li], state_rwkv_shift[li], prm, g, b, 512, 64, ts)
            s_s.append(s_new)
            sh_s.append(sh_new)
        elif kind == 1:
            prm = (b_w_in[li], b_w_grp[li], b_scale[li], b_w_out[li])
            xp, buf_new = _pool_layer(xp, bp, tp, jnp.zeros((bp, B_BUF, B_WIDTH), F32), False, prm, g, b, 512, 512)
            pl_p.append(buf_new)
            xs, buf_new = _pool_layer(xs, bs, ts, state_pool[li], True, prm, g, b, 512, ts)
            pl_s.append(buf_new)
        else:
            prm = (c_w_in[li], c_cmp_wk[li], c_cmp_wv[li], c_w_out[li])
            xp, rows, wk, wv = _nsa_prompt_layer(xp, bp, tp, prm, g, b)
            rows_p.append(rows)
            wk_p.append(wk)
            wv_p.append(wv)
            caches = (cache_cmp_k[li], cache_cmp_v[li], cache_sel_k[li], cache_sel_v[li])
            xs, rows, wk, wv = _nsa_sample_layer(xs, bs, ts, caches, page_table, state_win_k[li], state_win_v[li],
                                                 prm, g, b)
            rows_s.append(rows)
            wk_s.append(wk)
            wv_s.append(wv)
    stack = jnp.stack
    return (xp.reshape(bp, tp, D_MODEL), xs.reshape(bs, ts, D_MODEL), stack(s_p), stack(s_s), stack(sh_p), stack(sh_s),
            stack(pl_p), stack(pl_s),
            stack([r[0] for r in rows_p]), stack([r[0] for r in rows_s]),
            stack([r[1] for r in rows_p]), stack([r[1] for r in rows_s]),
            stack([r[2] for r in rows_p]), stack([r[2] for r in rows_s]),
            stack([r[3] for r in rows_p]), stack([r[3] for r in rows_s]),
            stack(wk_p), stack(wk_s), stack(wv_p), stack(wv_s))
```

```python
import functools

import jax
import jax.numpy as jnp
import numpy as np
from jax import lax
from jax.experimental import pallas as pl
from jax.experimental.pallas import tpu as pltpu

F32 = jnp.float32
BF16 = jnp.bfloat16

D_MODEL = 1024
DEPTH = 4
DEEPNORM_ALPHA = (2.0 * DEPTH) ** 0.25
LN_EPS = 1e-5

A_HEADS = 16
A_HEAD_DIM = 64
A_WIDTH = 1024
A_LORA = 64
A_NCOLS = 4 * A_WIDTH + 2 * A_LORA
A_GN_EPS = 64e-5
RWKV_CHUNK = 64
RWKV_ROWS_PER_STEP = 4

B_WIDTH = 1024
B_GROUP_W = 256
B_WINDOWS = (2, 4, 8, 16)
B_BUF = 15
B_CARRY = 16

C_HEADS = 16
C_KV_HEADS = 4
C_HPG = 4
C_HEAD_DIM = 64
C_WIDTH = 1024
C_KV_WIDTH = 256
C_CMP_BLOCK = 32
C_SEL_BLOCK = 64
C_TOP_N = 16
C_WINDOW = 512
C_FORCE = 1e9
PAGE_SIZE = 128

LANES = 128
VMEM_LIMIT = 56 * 1024 * 1024


def _cparams(n_axes):
    return pltpu.CompilerParams(dimension_semantics=("arbitrary",) * n_axes, vmem_limit_bytes=VMEM_LIMIT)


def _sigmoid(x):
    return 1.0 / (1.0 + jnp.exp(-x))


def _silu(x):
    return x * _sigmoid(x)


def _proj_kernel(x_ref, w_ref, *o_refs):
    acc = jnp.dot(x_ref[...].astype(BF16), w_ref[...], preferred_element_type=F32)
    lo = 0
    for o_ref in o_refs:
        width = o_ref.shape[1]
        o_ref[...] = acc[:, lo:lo + width]
        lo += width


def _proj(x, w, tm, widths=None):
    m, k = x.shape
    n = w.shape[1]
    widths = (n,) if widths is None else widths
    assert sum(widths) == n and all(wd % LANES == 0 for wd in widths)
    outs = pl.pallas_call(
        _proj_kernel,
        grid=(m // tm,),
        in_specs=[pl.BlockSpec((tm, k), lambda i: (i, 0)), pl.BlockSpec((k, n), lambda i: (0, 0))],
        out_specs=[pl.BlockSpec((tm, wd), lambda i: (i, 0)) for wd in widths],
        out_shape=[jax.ShapeDtypeStruct((m, wd), F32) for wd in widths],
        compiler_params=_cparams(1),
        name="proj",
    )(x, w)
    return outs[0] if len(widths) == 1 else outs


def _outproj_ln_kernel(y_ref, z_ref, x_ref, w_ref, g_ref, b_ref, o_ref):
    a = (y_ref[...] * _silu(z_ref[...])).astype(BF16)
    h = DEEPNORM_ALPHA * x_ref[...] + jnp.dot(a, w_ref[...], preferred_element_type=F32)
    mu = jnp.mean(h, axis=-1, keepdims=True)
    c = h - mu
    var = jnp.mean(c * c, axis=-1, keepdims=True)
    o_ref[...] = c * lax.rsqrt(var + LN_EPS) * g_ref[...] + b_ref[...]


def _outproj_ln(y, z_arr, z_col, x, w, g, b, tm):
    m = x.shape[0]
    row = lambda i: (i, 0)
    const = lambda i: (0, 0)
    return pl.pallas_call(
        _outproj_ln_kernel,
        grid=(m // tm,),
        in_specs=[
            pl.BlockSpec((tm, D_MODEL), row),
            pl.BlockSpec((tm, D_MODEL), lambda i: (i, z_col)),
            pl.BlockSpec((tm, D_MODEL), row),
            pl.BlockSpec((D_MODEL, D_MODEL), const),
            pl.BlockSpec((1, D_MODEL), const),
            pl.BlockSpec((1, D_MODEL), const),
        ],
        out_specs=pl.BlockSpec((tm, D_MODEL), row),
        out_shape=jax.ShapeDtypeStruct((m, D_MODEL), F32),
        compiler_params=_cparams(1),
        name="outproj_ln",
    )(y, z_arr, x, w, g.reshape(1, D_MODEL), b.reshape(1, D_MODEL))


def _rwkv_prep_kernel(p_ref, prev_ref, mu_ref, w0_ref, a0_ref, lora_ref, kk_ref, ka_ref,
                      r_o, w_o, k_o, v_o, kk_o, a_o, z_o, carry, *, tt, nb, log_decay):
    p = p_ref[...]
    row = lax.broadcasted_iota(jnp.int32, (tt, 1), 0)
    if nb == 1:
        @pl.when(pl.program_id(1) == 0)
        def _():
            carry[0:1, :] = prev_ref[0]

        before = carry[0:1, :]
        first = row == 0
    else:
        seq = tt // nb
        before = jnp.concatenate([jnp.broadcast_to(prev_ref[i], (seq, A_NCOLS)) for i in range(nb)], axis=0)
        first = (row & (seq - 1)) == 0
    p_shift = jnp.where(first, before, pltpu.roll(p, 1, axis=0))
    carry[0:1, :] = p[tt - 1:tt, :]
    pm = p + (p_shift - p) * mu_ref[...]
    r = pm[:, 0:A_WIDTH]
    k = pm[:, A_WIDTH:2 * A_WIDTH]
    v = pm[:, 2 * A_WIDTH:3 * A_WIDTH]
    z = pm[:, 3 * A_WIDTH:4 * A_WIDTH]
    lo = pm[:, 4 * A_WIDTH:A_NCOLS]
    is_w = lax.broadcasted_iota(jnp.int32, (1, 2 * A_LORA), 1) < A_LORA
    lo = jnp.where(is_w, jnp.tanh(lo), lo)
    lora = jnp.dot(lo.astype(BF16), lora_ref[...], preferred_element_type=F32)
    u = w0_ref[...] + lora[:, 0:A_WIDTH]
    w_log = jnp.minimum(u, 0.0) - jnp.log(1.0 + jnp.exp(-jnp.abs(u))) - 0.5
    a = _sigmoid(a0_ref[...] + lora[:, A_WIDTH:2 * A_WIDTH])
    r_o[...] = r
    w_o[...] = -jnp.exp(w_log) if log_decay else jnp.exp(-jnp.exp(w_log))
    k_o[...] = k * (1.0 + (a - 1.0) * ka_ref[...])
    v_o[...] = v
    kk_o[...] = k * kk_ref[...]
    a_o[...] = a
    z_o[...] = z


def _rwkv_prep(p, p_prev, mu, w0, a0, lora_w, k_k, k_a, batch, seq, tt, log_decay):
    nb = max(1, tt // seq)
    assert (seq % tt == 0 and nb == 1) or (tt % seq == 0 and batch % nb == 0 and seq & (seq - 1) == 0)
    nt = max(1, seq // tt)
    row = lambda b, t: (b * nt + t, 0)
    const = lambda b, t: (0, 0)
    vec = lambda a: a.reshape(1, -1)
    out = jax.ShapeDtypeStruct((batch * seq, A_WIDTH), F32)
    return pl.pallas_call(
        functools.partial(_rwkv_prep_kernel, tt=tt, nb=nb, log_decay=log_decay),
        grid=(batch // nb, nt),
        in_specs=[
            pl.BlockSpec((tt, A_NCOLS), row),
            pl.BlockSpec((nb, 1, A_NCOLS), lambda b, t: (b, 0, 0)),
            pl.BlockSpec((1, A_NCOLS), const),
            pl.BlockSpec((1, A_WIDTH), const),
            pl.BlockSpec((1, A_WIDTH), const),
            pl.BlockSpec((2 * A_LORA, 2 * A_WIDTH), const),
            pl.BlockSpec((1, A_WIDTH), const),
            pl.BlockSpec((1, A_WIDTH), const),
        ],
        out_specs=[pl.BlockSpec((tt, A_WIDTH), row)] * 7,
        out_shape=[out] * 7,
        scratch_shapes=[pltpu.VMEM((8, A_NCOLS), F32)],
        compiler_params=_cparams(2),
        name="rwkv_prep",
    )(p, p_prev.reshape(batch, 1, A_NCOLS), vec(mu), vec(w0), vec(a0), lora_w, vec(k_k), vec(k_a))


def _rwkv_scan_kernel(r_ref, w_ref, k_ref, v_ref, kkr_ref, a_ref, s0_ref, rk_ref, lg_ref, lb_ref,
                      y_ref, sout_ref, state, kk_s, b_s, wr_s, *, tt):
    t = pl.program_id(1)
    n = A_HEAD_DIM

    @pl.when(t == 0)
    def _():
        state[...] = s0_ref[...]

    kkr = kkr_ref[...]
    norm2 = jnp.sum(kkr * kkr, axis=1, keepdims=True)
    kk = kkr * lax.rsqrt(jnp.maximum(norm2, 1e-24))
    kk_s[...] = kk
    b_s[...] = kk * a_ref[...]
    wr_s[...] = w_ref[...] * r_ref[...]
    rk = rk_ref[...]
    lg = lg_ref[...]
    lb = lb_ref[...]

    def step(s, carry):
        sa = jnp.zeros((n, LANES), F32)
        y = jnp.zeros((n, LANES), F32)
        for j in range(n):
            sj = state[j]
            sa = sa - sj * kk_s[s, j:j + 1, :]
            y = y + sj * wr_s[s, j:j + 1, :]
        v = v_ref[s]
        for j in range(n):
            state[j] = state[j] * w_ref[s, j:j + 1, :] + sa * b_s[s, j:j + 1, :] + v * k_ref[s, j:j + 1, :]
        r = r_ref[s]
        k = k_ref[s]
        b_r = jnp.sum(b_s[s] * r, axis=0, keepdims=True)
        k_r = jnp.sum(k * r, axis=0, keepdims=True)
        bonus = jnp.sum(k * r * rk, axis=0, keepdims=True)
        y = y + sa * b_r + v * k_r
        mean = jnp.mean(y, axis=0, keepdims=True)
        c = y - mean
        var = jnp.mean(c * c, axis=0, keepdims=True)
        y_ref[s] = c * lax.rsqrt(var + A_GN_EPS) * lg + lb + bonus * v
        return carry

    lax.fori_loop(0, tt, step, 0)

    @pl.when(t == pl.num_programs(1) - 1)
    def _():
        sout_ref[...] = state[...]


def _rwkv_scan(r, w, k, v, kkr, a, s0, rk, lg, lb, tt):
    seq, n, chains = r.shape
    seq_blk = pl.BlockSpec((tt, n, LANES), lambda c, t: (t, 0, c))
    st_blk = pl.BlockSpec((n, n, LANES), lambda c, t: (0, 0, c))
    par_blk = pl.BlockSpec((n, LANES), lambda c, t: (0, c))
    return pl.pallas_call(
        functools.partial(_rwkv_scan_kernel, tt=tt),
        grid=(chains // LANES, seq // tt),
        in_specs=[seq_blk] * 6 + [st_blk] + [par_blk] * 3,
        out_specs=[seq_blk, st_blk],
        out_shape=[jax.ShapeDtypeStruct((seq, n, chains), F32), jax.ShapeDtypeStruct((n, n, chains), F32)],
        scratch_shapes=[pltpu.VMEM((n, n, LANES), F32)] + [pltpu.VMEM((tt, n, LANES), F32)] * 3,
        compiler_params=_cparams(2),
        name="rwkv_scan",
    )(r, w, k, v, kkr, a, s0, rk, lg, lb)


def _split3(x):
    hi = x.astype(BF16)
    rest = x - hi.astype(F32)
    mid = rest.astype(BF16)
    return hi, mid, (rest - mid.astype(F32)).astype(BF16)


def _dot_sel(x, sel):
    hi, mid, lo = _split3(x)
    d = lambda a: jnp.dot(a, sel, preferred_element_type=F32)
    return d(hi) + (d(mid) + d(lo))


def _rwkv_chunk_kernel(r_ref, lw_ref, k_ref, v_ref, kkr_ref, a_ref, s0_ref, rk_ref, lg_ref, lb_ref,
                       y_ref, sout_ref, state, *, ln, bb):
    c = pl.program_id(1)
    half = A_HEAD_DIM
    n_pairs = A_WIDTH // LANES

    @pl.when(c == 0)
    def _():
        state[...] = s0_ref[...]

    row = lax.broadcasted_iota(jnp.int32, (ln, 1), 0)
    lo = lax.broadcasted_iota(jnp.int32, (1, LANES), 1) < half
    r2 = lax.broadcasted_iota(jnp.int32, (2 * ln, ln), 0) & (ln - 1)
    c2 = lax.broadcasted_iota(jnp.int32, (2 * ln, ln), 1)
    strict2, incl2 = c2 < r2, c2 <= r2
    rr = lax.broadcasted_iota(jnp.int32, (LANES, LANES), 0)
    cc = lax.broadcasted_iota(jnp.int32, (LANES, LANES), 1)
    same_head = (rr & half) == (cc & half)
    seg = jnp.where(same_head, 1.0, 0.0).astype(BF16)
    rb = lax.broadcasted_iota(jnp.int32, (2 * ln, 2 * ln), 0)
    cb = lax.broadcasted_iota(jnp.int32, (2 * ln, 2 * ln), 1)
    strict_bd = ((cb & (ln - 1)) + jnp.where((rb & ln) == (cb & ln), 0, ln)) < (rb & (ln - 1))
    split = lambda x: jnp.concatenate([jnp.where(lo, x, 0.0), jnp.where(lo, 0.0, x)], axis=0)
    merge = lambda x2: jnp.where(lo, x2[0:ln], x2[ln:2 * ln])
    mm = lambda a, b: jnp.dot(a.astype(BF16), b.astype(BF16), preferred_element_type=F32)

    pairs = range(bb * n_pairs)
    sls = [slice((p % n_pairs) * LANES, (p % n_pairs + 1) * LANES) for p in pairs]
    rws = [slice((p // n_pairs) * ln, (p // n_pairs + 1) * ln) for p in pairs]
    load = lambda ref: [ref[rw, sl] for rw, sl in zip(rws, sls)]
    r, lw, k, v, kkr, a = (load(ref) for ref in (r_ref, lw_ref, k_ref, v_ref, kkr_ref, a_ref))
    norm2 = [_dot_sel(x * x, seg) for x in kkr]
    kk = [x * lax.rsqrt(jnp.maximum(n2, 1e-24)) for x, n2 in zip(kkr, norm2)]
    bv = [x * y for x, y in zip(kk, a)]
    cum = lw
    d = 1
    while d < ln:
        cum = [x + jnp.where(row >= d, pltpu.roll(x, d, axis=0), 0.0) for x in cum]
        d *= 2
    tot = [x[ln - 1:ln, :] for x in cum]
    kkd = [x * jnp.exp(cm - l) for x, cm, l in zip(kk, cum, lw)]
    rd = [x * jnp.exp(cm) for x, cm in zip(r, cum)]
    w_inv = [jnp.exp(-cm) for cm in cum]
    kd = [x * w for x, w in zip(k, w_inv)]
    bd = [x * w for x, w in zip(bv, w_inv)]
    s_old = [state[p // n_pairs, p % n_pairs] for p in pairs]

    xk = [split(x).astype(BF16) for x in kkd]
    xr = [split(x).astype(BF16) for x in rd]
    nil = [jnp.where(strict_bd, -_dot_t(x, split(y).astype(BF16)), 0.0) for x, y in zip(xk, bd)]
    gk = [_dot_t(jnp.concatenate([x, y], axis=0), z.astype(BF16)) for x, y, z in zip(xk, xr, kd)]
    sx = [_dot_t(jnp.concatenate([x, y], axis=0).astype(BF16), s.astype(BF16)) for x, y, s in zip(kkd, rd, s_old)]
    av = [mm(jnp.concatenate([jnp.where(strict2, g[0:2 * ln], 0.0), jnp.where(incl2, g[2 * ln:4 * ln], 0.0)], axis=0), x)
          for g, x in zip(gk, v)]
    u2 = [split(s[0:ln] + merge(x[0:2 * ln])) for s, x in zip(sx, av)]
    power = nil
    step = 1
    while step < ln:
        u2 = [x + mm(pw, x) for x, pw in zip(u2, power)]
        step *= 2
        if step < ln:
            power = [mm(pw, pw) for pw in power]
    u = [x[0:ln] + x[ln:2 * ln] for x in u2]
    a_rb = [jnp.where(incl2, _dot_t(x, y.astype(BF16)), 0.0) for x, y in zip(xr, bd)]
    y = [s[ln:2 * ln] + merge(x[2 * ln:4 * ln]) - merge(mm(g, w)) for s, x, g, w in zip(sx, av, a_rb, u)]

    inv_n = 1.0 / half
    for p in pairs:
        sl = sls[p]
        w_rest = jnp.exp(tot[p] - cum[p])
        upd = mm(jnp.concatenate([v[p], u[p]], axis=0).T,
                 jnp.concatenate([k[p] * w_rest, -(bv[p] * w_rest)], axis=0))
        state[p // n_pairs, p % n_pairs] = s_old[p] * jnp.exp(tot[p]) + jnp.where(same_head, upd, 0.0)
        mean = _dot_sel(y[p], seg) * inv_n
        cen = y[p] - mean
        var = _dot_sel(cen * cen, seg) * inv_n
        bonus = _dot_sel(r[p] * k[p] * rk_ref[:, sl], seg)
        y_ref[rws[p], sl] = cen * lax.rsqrt(var + A_GN_EPS) * lg_ref[:, sl] + lb_ref[:, sl] + bonus * v[p]

    @pl.when(c == pl.num_programs(1) - 1)
    def _():
        sout_ref[...] = state[...]


def _rwkv_chunked(r, lw, k, v, kkr, a, s0, r_k, lnx_g, lnx_b, batch, seq, ln):
    nc = seq // ln
    n_pairs = A_WIDTH // LANES
    bb = RWKV_ROWS_PER_STEP if nc == 1 and batch % RWKV_ROWS_PER_STEP == 0 else 1
    row = pl.BlockSpec((bb * ln, A_WIDTH), lambda b, c: (b * nc + c, 0))
    st = pl.BlockSpec((bb, n_pairs, LANES, LANES), lambda b, c: (b, 0, 0, 0))
    par = pl.BlockSpec((1, A_WIDTH), lambda b, c: (0, 0))
    return pl.pallas_call(
        functools.partial(_rwkv_chunk_kernel, ln=ln, bb=bb),
        grid=(batch // bb, nc),
        in_specs=[row] * 6 + [st] + [par] * 3,
        out_specs=[row, st],
        out_shape=[jax.ShapeDtypeStruct((batch * seq, A_WIDTH), F32),
                   jax.ShapeDtypeStruct((batch, n_pairs, LANES, LANES), F32)],
        scratch_shapes=[pltpu.VMEM((bb, n_pairs, LANES, LANES), F32)],
        compiler_params=_cparams(2),
        name="rwkv_chunk",
    )(r, lw, k, v, kkr, a, s0, r_k.reshape(1, A_WIDTH), lnx_g.reshape(1, A_WIDTH), lnx_b.reshape(1, A_WIDTH))


def _rwkv_layer(x, batch, seq, s0, p_prev, prm, ln_g, ln_b, tm, tt_prep, tt_scan):
    (w_in, mu, w0, w2, a0, a2, k_k, k_a, r_k, lnx_g, lnx_b, w_out) = prm
    h, n = A_HEADS, A_HEAD_DIM
    ln = min(RWKV_CHUNK, seq)
    chunked = seq % ln == 0 and ln % 8 == 0 and ln & (ln - 1) == 0
    p = _proj(x, w_in.astype(BF16), tm)
    zeros = jnp.zeros((A_LORA, A_WIDTH), F32)
    lora_w = jnp.concatenate([jnp.concatenate([w2, zeros], axis=1), jnp.concatenate([zeros, a2], axis=1)], axis=0)
    r, w, k, v, kkr, a, z = _rwkv_prep(p, p_prev, mu, w0, a0, lora_w.astype(BF16), k_k, k_a, batch, seq, tt_prep,
                                       chunked)
    p_last = p.reshape(batch, seq, A_NCOLS)[:, -1]
    if chunked:
        eye2 = jnp.eye(2, dtype=F32)
        s0_bd = (s0.reshape(batch, h // 2, 2, n, 1, n) * eye2[None, None, :, None, :, None]).reshape(
            batch, h // 2, 2 * n, 2 * n)
        y, s_bd = _rwkv_chunked(r, w, k, v, kkr, a, s0_bd, r_k, lnx_g, lnx_b, batch, seq, ln)
        s_bd = s_bd.reshape(batch, h // 2, 2, n, 2, n)
        s_final = jnp.stack([s_bd[:, :, 0, :, 0, :], s_bd[:, :, 1, :, 1, :]], axis=2).reshape(batch, h, n, n)
        x_new = _outproj_ln(y, z, 0, x, w_out.astype(BF16), ln_g, ln_b, tm)
        return x_new, s_final, p_last
    chains = batch * h
    to_scan = lambda t: t.reshape(batch, seq, h, n).transpose(1, 3, 0, 2).reshape(seq, n, chains)
    per_chain = lambda t: jnp.broadcast_to(t.reshape(h, n).T[:, None, :], (n, batch, h)).reshape(n, chains)
    s0_t = s0.transpose(3, 2, 0, 1).reshape(n, n, chains)
    y, s_out = _rwkv_scan(to_scan(r), to_scan(w), to_scan(k), to_scan(v), to_scan(kkr), to_scan(a), s0_t,
                          per_chain(r_k), per_chain(lnx_g), per_chain(lnx_b), tt_scan)
    y = y.reshape(seq, n, batch, h).transpose(2, 0, 3, 1).reshape(batch * seq, A_WIDTH)
    x_new = _outproj_ln(y, z, 0, x, w_out.astype(BF16), ln_g, ln_b, tm)
    s_final = s_out.reshape(n, n, batch, h).transpose(2, 3, 1, 0)
    return x_new, s_final, p_last


def _pool_kernel(u_ref, buf_ref, wg_ref, scale_ref, y_ref, nbuf_ref, ext, *, tt, front_valid):
    t = pl.program_id(1)

    @pl.when(t == 0)
    def _():
        ext[1:B_CARRY, :] = buf_ref[0]

    @pl.when(t > 0)
    def _():
        ext[0:B_CARRY, :] = ext[tt:tt + B_CARRY, :]

    u = u_ref[...]
    ext[B_CARRY:B_CARRY + tt, :] = u
    pos = t * tt + lax.broadcasted_iota(jnp.int32, (tt, 1), 0)
    for gi, win in enumerate(B_WINDOWS):
        lo, hi = gi * B_GROUP_W, (gi + 1) * B_GROUP_W
        ug = u[:, lo:hi]
        acc = ug
        for s in range(1, win):
            acc = acc + ext[B_CARRY - s:B_CARRY - s + tt, lo:hi]
        if front_valid:
            cnt = float(win)
        else:
            cnt = jnp.minimum(pos + 1, win).astype(F32)
        d = acc / cnt - ug
        yg = jnp.dot(d.astype(BF16), wg_ref[gi], preferred_element_type=F32)
        y_ref[:, lo:hi] = yg * scale_ref[:, lo:hi]

    @pl.when(t == pl.num_programs(1) - 1)
    def _():
        nbuf_ref[0] = ext[tt + 1:tt + B_CARRY, :]


def _pool(uz, buf, w_grp, scale, batch, seq, tt, front_valid):
    nt = seq // tt
    return pl.pallas_call(
        functools.partial(_pool_kernel, tt=tt, front_valid=front_valid),
        grid=(batch, nt),
        in_specs=[
            pl.BlockSpec((tt, B_WIDTH), lambda b, t: (b * nt + t, 0)),
            pl.BlockSpec((1, B_BUF, B_WIDTH), lambda b, t: (b, 0, 0)),
            pl.BlockSpec((len(B_WINDOWS), B_GROUP_W, B_GROUP_W), lambda b, t: (0, 0, 0)),
            pl.BlockSpec((1, B_WIDTH), lambda b, t: (0, 0)),
        ],
        out_specs=[
            pl.BlockSpec((tt, B_WIDTH), lambda b, t: (b * nt + t, 0)),
            pl.BlockSpec((1, B_BUF, B_WIDTH), lambda b, t: (b, 0, 0)),
        ],
        out_shape=[jax.ShapeDtypeStruct((batch * seq, B_WIDTH), F32),
                   jax.ShapeDtypeStruct((batch, B_BUF, B_WIDTH), F32)],
        scratch_shapes=[pltpu.VMEM((tt + B_CARRY, B_WIDTH), F32)],
        compiler_params=_cparams(2),
        name="pool",
    )(uz, buf, w_grp, scale.reshape(1, B_WIDTH))


def _pool_layer(x, batch, seq, buf, front_valid, prm, ln_g, ln_b, tm, tt):
    w_in, w_grp, scale, w_out = prm
    uz = _proj(x, w_in.astype(BF16), tm)
    y, new_buf = _pool(uz, buf, w_grp.astype(BF16), scale, batch, seq, tt, front_valid)
    x_new = _outproj_ln(y, uz, 1, x, w_out.astype(BF16), ln_g, ln_b, tm)
    return x_new, new_buf


NEG = -1e30
C_SCALE = C_HEAD_DIM ** -0.5


def _alibi_slopes():
    return jnp.power(2.0, -8.0 * (jnp.arange(C_HEADS, dtype=F32) + 1.0) / C_HEADS)


def _dot_t(a, b):
    return lax.dot_general(a, b, (((1,), (1,)), ((), ())), preferred_element_type=F32)


def _masked_softmax_parts(scores, masks):
    top = None
    for s, k in zip(scores, masks):
        part = jnp.where(k, s, -jnp.inf)
        top = part if top is None else jnp.maximum(top, part)
    m = jnp.max(top, axis=1, keepdims=True)
    m = jnp.where(jnp.isfinite(m), m, 0.0)
    es = [jnp.where(k, jnp.exp(s - m), 0.0) for s, k in zip(scores, masks)]
    total = es[0]
    for e in es[1:]:
        total = total + e
    den = jnp.maximum(jnp.sum(total, axis=1, keepdims=True), 1e-30)
    return [e / den for e in es]


def _flash_init(rows, width):
    return (jnp.full((rows, 1), NEG, F32), jnp.zeros((rows, 1), F32), jnp.zeros((rows, width), F32))


def _flash_update(carry, qb, kb, vb, bias):
    m, l, acc = carry
    s, mask = bias(_dot_t(qb, kb))
    s = jnp.where(mask, s, NEG)
    m_new = jnp.maximum(m, jnp.max(s, axis=1, keepdims=True))
    alpha = jnp.exp(m - m_new)
    p = jnp.where(mask, jnp.exp(s - m_new), 0.0)
    l = alpha * l + jnp.sum(p, axis=1, keepdims=True)
    acc = alpha * acc + jnp.dot(p.astype(BF16), vb, preferred_element_type=F32)
    return m_new, l, acc


def _flash_finish(carry):
    _, l, acc = carry
    return acc / jnp.maximum(l, 1e-30)


def _top_n_mask(imp, top_n):
    n = imp.shape[1]
    idx = lax.broadcasted_iota(jnp.int32, (1, n), 1)
    rank = jnp.zeros(imp.shape, F32)
    for c in range(n):
        col = imp[:, c:c + 1]
        rank = rank + jnp.where(idx > c, jnp.where(col >= imp, 1.0, 0.0), jnp.where(col > imp, 1.0, 0.0))
    return jnp.where(rank < top_n, jnp.where(imp >= 0.0, 1.0, 0.0), 0.0)


def _top_n_mask_wide(imp, top_n):
    n = imp.shape[1]
    shift = n.bit_length() - 1
    assert n == 1 << shift
    lane = lax.broadcasted_iota(jnp.int32, (n, n * n), 1)
    src = lax.broadcasted_iota(jnp.int32, (n, n * n), 0)
    from_cand = jnp.where(src == lax.shift_right_logical(lane, shift), 1.0, 0.0).astype(BF16)
    from_entry = jnp.where(src == (lane & (n - 1)), 1.0, 0.0).astype(BF16)
    cand = _dot_sel(imp, from_cand)
    entry = _dot_sel(imp, from_entry)
    c_idx = lax.shift_right_logical(lane[0:1], shift)
    s_idx = lane[0:1] & (n - 1)
    beats = jnp.where(s_idx > c_idx, jnp.where(cand >= entry, 1.0, 0.0), jnp.where(cand > entry, 1.0, 0.0))
    rank = _dot_t(beats.astype(BF16), from_entry)
    return jnp.where(rank < top_n, jnp.where(imp >= 0.0, 1.0, 0.0), 0.0)


def _block_expand(n_blocks, kpos):
    blk = lax.broadcasted_iota(jnp.int32, (n_blocks, kpos.shape[1]), 0)
    return jnp.where(lax.shift_right_logical(kpos, 6) == blk, 1.0, 0.0).astype(BF16)


def _cmp_kernel(kv_ref, w_ref, e_ref, o_ref):
    x = kv_ref[...]
    n_pair = x.shape[0] // (2 * C_CMP_BLOCK)
    x4 = x.reshape(n_pair, 2, C_CMP_BLOCK, x.shape[1])
    w = w_ref[...][None]
    e_ref[0] = jnp.sum(x4[:, 0] * w, axis=1)
    o_ref[0] = jnp.sum(x4[:, 1] * w, axis=1)


def _cmp(p, col_block, w, batch, seq):
    width = w.shape[1]
    n_pair = seq // (2 * C_CMP_BLOCK)
    out = jax.ShapeDtypeStruct((batch, n_pair, width), F32)
    return pl.pallas_call(
        _cmp_kernel,
        grid=(batch,),
        in_specs=[pl.BlockSpec((seq, width), lambda b: (b, col_block)), pl.BlockSpec(w.shape, lambda b: (0, 0))],
        out_specs=[pl.BlockSpec((1, n_pair, width), lambda b: (b, 0, 0))] * 2,
        out_shape=[out, out],
        compiler_params=_cparams(1),
        name="nsa_cmp",
    )(p, w)


def _nsa_prompt_kernel(q_ref, g_ref, ce_ref, co_ref, sel_ref, win_ref, slope_ref, o_ref, *, tq, tk):
    qi = pl.program_id(2)
    q0 = qi * tq
    hd = C_HEAD_DIM
    rows = C_HPG * tq
    n_pair = ce_ref.shape[1]
    stack = lambda f: jnp.concatenate([f(j) for j in range(C_HPG)], axis=0)
    tile4 = lambda x: jnp.concatenate([x] * C_HPG, axis=0)
    lo = lax.broadcasted_iota(jnp.int32, (1, LANES), 1) < hd
    q_all = q_ref[...]

    def q_head(j):
        pair_tile = q_all[:, (j // 2) * LANES:(j // 2 + 1) * LANES]
        return jnp.where(lo, pair_tile if j % 2 == 0 else pltpu.roll(pair_tile, hd, axis=1), 0.0)

    qb = (stack(q_head) * C_SCALE).astype(BF16)
    slope = stack(lambda j: jnp.broadcast_to(slope_ref[0, j:j + 1, 0:1], (tq, 1)))
    slope_keys = jnp.broadcast_to(slope, (rows, tk))
    qpos_t = q0 + lax.broadcasted_iota(jnp.int32, (tq, 1), 0)
    qpos = tile4(qpos_t)

    n_gate = 3 * C_HPG
    src = lax.broadcasted_iota(jnp.int32, (LANES, n_gate * LANES), 0)
    dst = lax.shift_right_logical(lax.broadcasted_iota(jnp.int32, (LANES, n_gate * LANES), 1), 7)
    gates = _dot_sel(_sigmoid(g_ref[...]), jnp.where(src == dst, 1.0, 0.0).astype(BF16))
    gate = lambda br: stack(lambda j: gates[:, (3 * j + br) * LANES:(3 * j + br + 1) * LANES])

    pair = lax.broadcasted_iota(jnp.int32, (1, n_pair), 1)
    ce, co = ce_ref[0].astype(BF16), co_ref[0].astype(BF16)
    scores, masks = [], []
    for par, c in ((0, ce), (1, co)):
        dist = qpos - ((2 * pair + par + 1) * C_CMP_BLOCK - 1)
        scores.append(_dot_t(qb, c) - slope * dist.astype(F32))
        masks.append(dist >= 0)
    p_e, p_o = _masked_softmax_parts(scores, masks)
    o_c = (jnp.dot(p_e.astype(BF16), ce, preferred_element_type=F32)
           + jnp.dot(p_o.astype(BF16), co, preferred_element_type=F32))
    imp_h = p_e + p_o
    imp = imp_h[0:tq]
    for j in range(1, C_HPG):
        imp = imp + imp_h[j * tq:(j + 1) * tq]
    cur = lax.shift_right_logical(qpos_t, 6)
    imp = jnp.where(pair == cur, C_FORCE, imp)
    imp = jnp.where(pair <= cur, imp, -1.0)
    sel = _top_n_mask_wide(imp, C_TOP_N).astype(BF16)

    def attend(carry, k0, kv, penalty_of):
        m, acc = carry
        kpos = k0 + lax.broadcasted_iota(jnp.int32, (1, tk), 1)
        dist = qpos_t - kpos
        s = (_dot_t(qb, kv.astype(BF16)) - slope_keys * tile4(dist.astype(F32))) + tile4(penalty_of(dist, kpos))
        m_new = jnp.maximum(m, jnp.max(s, axis=1, keepdims=True))
        p = jnp.exp(s - m_new).astype(BF16)
        ones_v = jnp.where(lo, 1.0, kv).astype(BF16)
        acc = jnp.exp(m - m_new) * acc + jnp.dot(p, ones_v, preferred_element_type=F32)
        return m_new, acc

    def finish(carry):
        _, acc = carry
        return acc / jnp.where(lo, 1.0, jnp.maximum(pltpu.roll(acc, hd, axis=1), 1e-30))

    init = (jnp.full((rows, 1), NEG, F32), jnp.zeros((rows, LANES), F32))

    def sel_step(c, carry):
        k0 = pl.multiple_of(c * tk, tk)

        def penalty(dist, kpos):
            chosen = jnp.dot(sel, _block_expand(n_pair, kpos), preferred_element_type=F32)
            return jnp.where(dist >= 0, jnp.where(chosen > 0.5, 0.0, NEG), NEG)

        return attend(carry, k0, sel_ref[pl.ds(k0, tk), :], penalty)

    o_s = finish(lax.fori_loop(0, qi, sel_step, sel_step(qi, init)))

    w0 = jnp.maximum(q0 - C_WINDOW, 0)
    n_win = (tq + C_WINDOW) // tk
    own = (q0 - w0) // tk

    def win_step(i, carry):
        k0 = pl.multiple_of(w0 + lax.rem(own + i, n_win) * tk, tk)
        penalty = lambda dist, kpos: jnp.where(dist >= 0, jnp.where(dist < C_WINDOW, 0.0, NEG), NEG)
        return attend(carry, k0, win_ref[pl.ds(k0, tk), :], penalty)

    o_w = finish(lax.fori_loop(1, n_win, win_step, win_step(0, init)))

    o = gate(0) * o_c + gate(1) * o_s + gate(2) * o_w
    head = lambda j: o[j * tq:(j + 1) * tq]
    o_ref[...] = jnp.concatenate([jnp.where(lo, pltpu.roll(head(2 * i), hd, axis=1), head(2 * i + 1))
                                  for i in range(C_HPG // 2)], axis=1)


def _nsa_prompt_layer(x, batch, seq, prm, ln_g, ln_b):
    w_in, cmp_wk, cmp_wv, w_out = prm
    hd, grp = C_HEAD_DIM, C_KV_HEADS
    tq = tk = 256
    assert seq % tq == 0 and seq >= tq + C_WINDOW and C_WINDOW % tk == 0 and seq % (2 * C_CMP_BLOCK) == 0
    assert tq == tk and seq == 2 * C_CMP_BLOCK * (seq // C_SEL_BLOCK)
    kv_w = lambda i: w_in[:, C_WIDTH + C_KV_WIDTH * i:C_WIDTH + C_KV_WIDTH * (i + 1)].reshape(D_MODEL, grp, hd)
    kv_pair = lambda a, b: jnp.concatenate([kv_w(a), kv_w(b)], axis=2).reshape(D_MODEL, grp * 2 * hd)
    g_lo = C_WIDTH + 6 * C_KV_WIDTH
    g_w = w_in[:, g_lo:g_lo + 3 * C_HEADS].reshape(D_MODEL, grp, 3 * C_HPG)
    g_w = jnp.pad(g_w, ((0, 0), (0, 0), (0, LANES - 3 * C_HPG))).reshape(D_MODEL, grp * LANES)
    w_p = jnp.concatenate([w_in[:, :C_WIDTH], w_in[:, g_lo + 3 * C_HEADS:], kv_pair(0, 1), kv_pair(2, 3),
                           kv_pair(4, 5), g_w, w_in[:, C_WIDTH:C_WIDTH + 4 * C_KV_WIDTH]], axis=1)
    n_main = 2 * C_WIDTH + 4 * grp * 2 * hd
    p, *new_rows = _proj(x, w_p.astype(BF16), 256, (n_main,) + (C_KV_WIDTH,) * 4)
    kv_lo = 2 * C_WIDTH
    kvw = grp * 2 * hd
    cw = jnp.concatenate([jnp.broadcast_to(cmp_wk[:, None], (C_CMP_BLOCK, hd)),
                          jnp.broadcast_to(cmp_wv[:, None], (C_CMP_BLOCK, hd))], axis=1)
    ce, co = _cmp(p, kv_lo // kvw, jnp.tile(cw, (1, grp)), batch, seq)
    slopes = jnp.broadcast_to(jnp.pad(_alibi_slopes().reshape(grp, C_HPG), ((0, 0), (0, 8 - C_HPG)))[:, :, None],
                              (grp, 8, LANES))
    nq = seq // tq
    n_pair = seq // (2 * C_CMP_BLOCK)
    gw = 2 * hd
    o = pl.pallas_call(
        functools.partial(_nsa_prompt_kernel, tq=tq, tk=tk),
        grid=(batch, grp, nq),
        in_specs=[
            pl.BlockSpec((tq, C_HPG * hd), lambda b, g, i: (b * nq + i, g)),
            pl.BlockSpec((tq, LANES), lambda b, g, i: (b * nq + i, (kv_lo + 3 * kvw) // LANES + g)),
            pl.BlockSpec((1, n_pair, gw), lambda b, g, i: (b, 0, g)),
            pl.BlockSpec((1, n_pair, gw), lambda b, g, i: (b, 0, g)),
            pl.BlockSpec((seq, gw), lambda b, g, i: (b, (kv_lo + kvw) // gw + g)),
            pl.BlockSpec((seq, gw), lambda b, g, i: (b, (kv_lo + 2 * kvw) // gw + g)),
            pl.BlockSpec((1, 8, LANES), lambda b, g, i: (g, 0, 0)),
        ],
        out_specs=pl.BlockSpec((tq, C_HPG * hd), lambda b, g, i: (b * nq + i, g)),
        out_shape=jax.ShapeDtypeStruct((batch * seq, C_WIDTH), F32),
        compiler_params=_cparams(3),
        name="nsa_prompt",
    )(p, p, ce, co, p, p, slopes)
    x_new = _outproj_ln(o, p, 1, x, w_out.astype(BF16), ln_g, ln_b, 512)
    rows = tuple(r.reshape(batch, seq, grp, hd) for r in new_rows)
    keep = min(C_WINDOW, seq)
    win = p.reshape(batch, seq, -1)[:, seq - keep:, kv_lo + 2 * kvw:kv_lo + 3 * kvw].reshape(batch, keep, grp, 2, hd)
    return x_new, rows, win[:, :, :, 0], win[:, :, :, 1]


def _nsa_sample_kernel(tbl_ref, *refs, n_pages, ts):
    del tbl_ref
    ck, cv, sk, sv = (refs[i * n_pages:(i + 1) * n_pages] for i in range(4))
    (q_ref, g_ref, ksn_ref, vsn_ref, kwn_ref, vwn_ref, wk_ref, wv_ref, cwk_ref, cwv_ref, slope_ref,
     o_ref, ke_s, ko_s, ve_s, vo_s, new_s) = refs[4 * n_pages:]
    hd, grp = C_HEAD_DIM, C_KV_HEADS
    rows = C_HEADS * ts
    past = n_pages * PAGE_SIZE
    n_pair = past // C_SEL_BLOCK
    qpos = past + lax.rem(lax.broadcasted_iota(jnp.int32, (rows, 1), 0), ts)
    qb = (q_ref[0] * C_SCALE).astype(BF16)
    slope = slope_ref[:, 0:1]

    per_page = PAGE_SIZE // C_CMP_BLOCK
    for pages, cw_ref, e_s, o_s in ((ck, cwk_ref, ke_s, ko_s), (cv, cwv_ref, ve_s, vo_s)):
        cw = cw_ref[...][None]
        for pg in range(n_pages):
            tok = jnp.sum(pages[pg][0].reshape(per_page, C_CMP_BLOCK, grp * hd) * cw, axis=1)
            for i in range(per_page):
                n = pg * per_page + i
                dst = e_s if n % 2 == 0 else o_s
                dst[n // 2:n // 2 + 1, :] = tok[i:i + 1]

    pair = lax.broadcasted_iota(jnp.int32, (1, n_pair), 1)
    scores, masks = [], []
    for par, k_s in ((0, ke_s), (1, ko_s)):
        dist = qpos - ((2 * pair + par + 1) * C_CMP_BLOCK - 1)
        scores.append(_dot_t(qb, k_s[...].astype(BF16)) - slope * dist.astype(F32))
        masks.append(dist >= 0)
    p_e, p_o = _masked_softmax_parts(scores, masks)
    o_c = (jnp.dot(p_e.astype(BF16), ve_s[...].astype(BF16), preferred_element_type=F32)
           + jnp.dot(p_o.astype(BF16), vo_s[...].astype(BF16), preferred_element_type=F32))
    imp_h = p_e + p_o
    sel_rows = []
    for g in range(grp):
        base = g * C_HPG * ts
        imp = imp_h[base:base + ts]
        for j in range(1, C_HPG):
            imp = imp + imp_h[base + j * ts:base + (j + 1) * ts]
        sel_g = _top_n_mask(imp, C_TOP_N - 1)
        sel_rows += [sel_g] * C_HPG
    sel_rows = jnp.concatenate(sel_rows, axis=0).astype(BF16)

    lane_pos = lax.broadcasted_iota(jnp.int32, (1, PAGE_SIZE), 1)

    def new_rows(slot, k_ref, v_ref):
        new_s[2 * slot:2 * slot + 2] = jnp.zeros((2,) + new_s.shape[1:], F32)
        new_s[2 * slot, 0:ts, :] = k_ref[...]
        new_s[2 * slot + 1, 0:ts, :] = v_ref[...]
        return new_s[2 * slot].astype(BF16), new_s[2 * slot + 1].astype(BF16)

    def softmax_attend(tiles):
        scores = []
        for kb, _, kpos, pen in tiles:
            dist = qpos - kpos
            scores.append((_dot_t(qb, kb) - slope * dist.astype(F32)) + pen(dist))
        top = scores[0]
        for s in scores[1:]:
            top = jnp.maximum(top, s)
        m = jnp.max(top, axis=1, keepdims=True)
        probs = [jnp.exp(s - m) for s in scores]
        total = probs[0]
        for p in probs[1:]:
            total = total + p
        acc = None
        for p, (_, vb, _, _) in zip(probs, tiles):
            part = jnp.dot(p.astype(BF16), vb, preferred_element_type=F32)
            acc = part if acc is None else acc + part
        return acc / jnp.maximum(jnp.sum(total, axis=1, keepdims=True), 1e-30)

    causal = lambda dist: jnp.where(dist >= 0, 0.0, NEG)
    window = lambda dist: jnp.where(dist >= 0, jnp.where(dist < C_WINDOW, 0.0, NEG), NEG)

    tiles = []
    for pg in range(n_pages):
        kpos = pg * PAGE_SIZE + lane_pos
        chosen = jnp.dot(sel_rows, _block_expand(n_pair, kpos), preferred_element_type=F32)
        pen = lambda dist, chosen=chosen: jnp.where(dist >= 0, jnp.where(chosen > 0.5, 0.0, NEG), NEG)
        tiles.append((sk[pg][0].astype(BF16), sv[pg][0].astype(BF16), kpos, pen))
    tiles.append(new_rows(0, ksn_ref, vsn_ref) + (past + lane_pos, causal))
    o_s = softmax_attend(tiles)

    n_win = wk_ref.shape[1]
    tiles = []
    for c in range(n_win // PAGE_SIZE):
        lo = c * PAGE_SIZE
        tiles.append((wk_ref[0, lo:lo + PAGE_SIZE, :].astype(BF16), wv_ref[0, lo:lo + PAGE_SIZE, :].astype(BF16),
                      past - n_win + lo + lane_pos, window))
    tiles.append(new_rows(1, kwn_ref, vwn_ref) + (past + lane_pos, window))
    o_w = softmax_attend(tiles)

    per_group = C_HPG * ts
    diag = lambda full: jnp.concatenate(
        [full[g * per_group:(g + 1) * per_group, g * hd:(g + 1) * hd] for g in range(grp)], axis=0)
    gates = _sigmoid(g_ref[0])
    o_ref[0] = gates[:, 0:1] * diag(o_c) + gates[:, 1:2] * diag(o_s) + gates[:, 2:3] * diag(o_w)


def _nsa_sample_layer(x, batch, ts, caches, page_table, win_k, win_v, prm, ln_g, ln_b):
    w_in, cmp_wk, cmp_wv, w_out = prm
    hd, grp = C_HEAD_DIM, C_KV_HEADS
    n_pages = page_table.shape[1]
    n_win = win_k.shape[1]
    past = n_pages * PAGE_SIZE
    rows = C_HEADS * ts
    assert past % C_SEL_BLOCK == 0 and ts <= C_SEL_BLOCK and ts % 8 == 0 and n_win == C_WINDOW and past >= n_win
    g_lo = C_WIDTH + 6 * C_KV_WIDTH
    w_s = jnp.concatenate([w_in[:, :C_WIDTH], w_in[:, g_lo + 3 * C_HEADS:], w_in[:, C_WIDTH:g_lo],
                           jnp.pad(w_in[:, g_lo:g_lo + 3 * C_HEADS], ((0, 0), (0, LANES - 3 * C_HEADS)))], axis=1)
    p = _proj(x, w_s.astype(BF16), 512)
    kv_lo = 2 * C_WIDTH
    q = p[:, :C_WIDTH].reshape(batch, ts, grp, C_HPG, hd).transpose(0, 2, 3, 1, 4).reshape(batch, grp, C_HPG * ts, hd)
    q_bd = (q[:, :, :, None, :] * jnp.eye(grp, dtype=F32)[None, :, None, :, None]).reshape(batch, rows, grp * hd)
    g_lo_p = kv_lo + 6 * C_KV_WIDTH
    g_t = p[:, g_lo_p:g_lo_p + 3 * C_HEADS].reshape(batch, ts, C_HEADS, 3).transpose(0, 2, 1, 3).reshape(batch, rows, 3)
    slope_rows = jnp.broadcast_to(jnp.repeat(_alibi_slopes(), ts)[:, None], (rows, LANES))
    bcast = lambda w: jnp.broadcast_to(w[:, None], (C_CMP_BLOCK, grp * hd))
    pools = [c.reshape(c.shape[0], PAGE_SIZE, grp * hd) for c in caches]
    page_spec = lambda pg: pl.BlockSpec((1, PAGE_SIZE, grp * hd), lambda b, tbl: (tbl[b, pg], 0, 0))
    new_spec = lambda i: pl.BlockSpec((ts, C_KV_WIDTH), lambda b, tbl: (b, kv_lo // C_KV_WIDTH + i))
    const2 = lambda shape: pl.BlockSpec(shape, lambda b, tbl: (0, 0))
    per_b = lambda shape: pl.BlockSpec((1,) + shape, lambda b, tbl: (b, 0, 0))
    in_specs = ([page_spec(pg) for _ in range(4) for pg in range(n_pages)]
                + [per_b((rows, grp * hd)), per_b((rows, 3)), new_spec(2), new_spec(3), new_spec(4), new_spec(5),
                   per_b((n_win, grp * hd)), per_b((n_win, grp * hd)),
                   const2((C_CMP_BLOCK, grp * hd)), const2((C_CMP_BLOCK, grp * hd)), const2((rows, LANES))])
    n_pair = past // C_SEL_BLOCK
    o = pl.pallas_call(
        functools.partial(_nsa_sample_kernel, n_pages=n_pages, ts=ts),
        grid_spec=pltpu.PrefetchScalarGridSpec(
            num_scalar_prefetch=1,
            grid=(batch,),
            in_specs=in_specs,
            out_specs=per_b((rows, hd)),
            scratch_shapes=[pltpu.VMEM((n_pair, grp * hd), F32)] * 4 + [pltpu.VMEM((4, PAGE_SIZE, grp * hd), F32)],
        ),
        out_shape=jax.ShapeDtypeStruct((batch, rows, hd), F32),
        compiler_params=_cparams(1),
        name="nsa_sample",
    )(page_table, *[pool for pool in pools for _ in range(n_pages)], q_bd, g_t, p, p, p, p,
      win_k.reshape(batch, n_win, grp * hd), win_v.reshape(batch, n_win, grp * hd), bcast(cmp_wk), bcast(cmp_wv),
      slope_rows)
    y = o.reshape(batch, C_HEADS, ts, hd).transpose(0, 2, 1, 3).reshape(batch * ts, C_WIDTH)
    x_new = _outproj_ln(y, p, 1, x, w_out.astype(BF16), ln_g, ln_b, 512)
    new = lambda i: p[:, kv_lo + i * C_KV_WIDTH:kv_lo + (i + 1) * C_KV_WIDTH].reshape(batch, ts, grp, hd)
    keep = lambda buf, i: jnp.concatenate([buf, new(i)], axis=1)[:, -n_win:]
    return x_new, (new(0), new(1), new(2), new(3)), keep(win_k, 4), keep(win_v, 5)


def kernel(x_prompt, x_sample, state_rwkv_S, state_rwkv_shift, state_pool, cache_cmp_k, cache_cmp_v, cache_sel_k,
           cache_sel_v, state_win_k, state_win_v, page_table, ln_g, ln_b, a_w_in, a_mu, a_w0, a_w2, a_a0, a_a2, a_k_k,
           a_k_a, a_r_k, a_lnx_g, a_lnx_b, a_w_out, b_w_in, b_w_grp, b_scale, b_w_out, c_w_in, c_cmp_wk, c_cmp_wv,
           c_w_out):
    bp, tp, _ = x_prompt.shape
    bs, ts, _ = x_sample.shape
    xp = x_prompt.reshape(bp * tp, D_MODEL)
    xs = x_sample.reshape(bs * ts, D_MODEL)
    s_p, s_s, sh_p, sh_s, pl_p, pl_s = [], [], [], [], [], []
    rows_p, rows_s, wk_p, wk_s, wv_p, wv_s = [], [], [], [], [], []
    for layer in range(DEPTH):
        kind, li = layer % 3, layer // 3
        g, b = ln_g[layer], ln_b[layer]
        if kind == 0:
            prm = (a_w_in[li], a_mu[li], a_w0[li], a_w2[li], a_a0[li], a_a2[li], a_k_k[li], a_k_a[li], a_r_k[li],
                   a_lnx_g[li], a_lnx_b[li], a_w_out[li])
            xp, s_new, sh_new = _rwkv_layer(xp, bp, tp, jnp.zeros((bp, A_HEADS, A_HEAD_DIM, A_HEAD_DIM), F32),
                                            jnp.zeros((bp, A_NCOLS), F32), prm, g, b, 512, 256, 32)
            s_p.append(s_new)
            sh_p.append(sh_new)
            xs, s_new, sh_new = _rwkv_layer(xs, bs, ts, state_rwkv_S[li], state_rwkv_shift[li], prm, g, b, 512, 64, ts)
            s_s.append(s_new)
            sh_s.append(sh_new)
        elif kind == 1:
            prm = (b_w_in[li], b_w_grp[li], b_scale[li], b_w_out[li])
            xp, buf_new = _pool_layer(xp, bp, tp, jnp.zeros((bp, B_BUF, B_WIDTH), F32), False, prm, g, b, 512, 512)
            pl_p.append(buf_new)
            xs, buf_new = _pool_layer(xs, bs, ts, state_pool[li], True, prm, g, b, 512, ts)
            pl_s.append(buf_new)
        else:
            prm = (c_w_in[li], c_cmp_wk[li], c_cmp_wv[li], c_w_out[li])
            xp, rows, wk, wv = _nsa_prompt_layer(xp, bp, tp, prm, g, b)
            rows_p.append(rows)
            wk_p.append(wk)
            wv_p.append(wv)
            caches = (cache_cmp_k[li], cache_cmp_v[li], cache_sel_k[li], cache_sel_v[li])
            xs, rows, wk, wv = _nsa_sample_layer(xs, bs, ts, caches, page_table, state_win_k[li], state_win_v[li],
                                                 prm, g, b)
            rows_s.append(rows)
            wk_s.append(wk)
            wv_s.append(wv)
    stack = jnp.stack
    return (xp.reshape(bp, tp, D_MODEL), xs.reshape(bs, ts, D_MODEL), stack(s_p), stack(s_s), stack(sh_p), stack(sh_s),
            stack(pl_p), stack(pl_s),
            stack([r[0] for r in rows_p]), stack([r[0] for r in rows_s]),
            stack([r[1] for r in rows_p]), stack([r[1] for r in rows_s]),
            stack([r[2] for r in rows_p]), stack([r[2] for r in rows_s]),
            stack([r[3] for r in rows_p]), stack([r[3] for r in rows_s]),
            stack(wk_p), stack(wk_s), stack(wv_p), stack(wv_s))
```

```python
import functools

import jax
import jax.numpy as jnp
import numpy as np
from jax import lax
from jax.experimental import pallas as pl
from jax.experimental.pallas import tpu as pltpu

F32 = jnp.float32
BF16 = jnp.bfloat16

D_MODEL = 1024
DEPTH = 4
DEEPNORM_ALPHA = (2.0 * DEPTH) ** 0.25
LN_EPS = 1e-5

A_HEADS = 16
A_HEAD_DIM = 64
A_WIDTH = 1024
A_LORA = 64
A_NCOLS = 4 * A_WIDTH + 2 * A_LORA
A_GN_EPS = 64e-5
RWKV_CHUNK = 64
RWKV_ROWS_PER_STEP = 4

B_WIDTH = 1024
B_GROUP_W = 256
B_WINDOWS = (2, 4, 8, 16)
B_BUF = 15
B_CARRY = 16

C_HEADS = 16
C_KV_HEADS = 4
C_HPG = 4
C_HEAD_DIM = 64
C_WIDTH = 1024
C_KV_WIDTH = 256
C_CMP_BLOCK = 32
C_SEL_BLOCK = 64
C_TOP_N = 16
C_WINDOW = 512
C_FORCE = 1e9
PAGE_SIZE = 128

LANES = 128
VMEM_LIMIT = 56 * 1024 * 1024


def _cparams(n_axes):
    return pltpu.CompilerParams(dimension_semantics=("arbitrary",) * n_axes, vmem_limit_bytes=VMEM_LIMIT)


def _sigmoid(x):
    return 1.0 / (1.0 + jnp.exp(-x))


def _silu(x):
    return x * _sigmoid(x)


def _proj_kernel(x_ref, w_ref, *o_refs):
    acc = jnp.dot(x_ref[...].astype(BF16), w_ref[...], preferred_element_type=F32)
    lo = 0
    for o_ref in o_refs:
        width = o_ref.shape[1]
        o_ref[...] = acc[:, lo:lo + width]
        lo += width


def _proj(x, w, tm, widths=None):
    m, k = x.shape
    n = w.shape[1]
    widths = (n,) if widths is None else widths
    assert sum(widths) == n and all(wd % LANES == 0 for wd in widths)
    outs = pl.pallas_call(
        _proj_kernel,
        grid=(m // tm,),
        in_specs=[pl.BlockSpec((tm, k), lambda i: (i, 0)), pl.BlockSpec((k, n), lambda i: (0, 0))],
        out_specs=[pl.BlockSpec((tm, wd), lambda i: (i, 0)) for wd in widths],
        out_shape=[jax.ShapeDtypeStruct((m, wd), F32) for wd in widths],
        compiler_params=_cparams(1),
        name="proj",
    )(x, w)
    return outs[0] if len(widths) == 1 else outs


def _outproj_ln_kernel(y_ref, z_ref, x_ref, w_ref, g_ref, b_ref, o_ref):
    a = (y_ref[...] * _silu(z_ref[...])).astype(BF16)
    h = DEEPNORM_ALPHA * x_ref[...] + jnp.dot(a, w_ref[...], preferred_element_type=F32)
    mu = jnp.mean(h, axis=-1, keepdims=True)
    c = h - mu
    var = jnp.mean(c * c, axis=-1, keepdims=True)
    o_ref[...] = c * lax.rsqrt(var + LN_EPS) * g_ref[...] + b_ref[...]


def _outproj_ln(y, z_arr, z_col, x, w, g, b, tm):
    m = x.shape[0]
    row = lambda i: (i, 0)
    const = lambda i: (0, 0)
    return pl.pallas_call(
        _outproj_ln_kernel,
        grid=(m // tm,),
        in_specs=[
            pl.BlockSpec((tm, D_MODEL), row),
            pl.BlockSpec((tm, D_MODEL), lambda i: (i, z_col)),
            pl.BlockSpec((tm, D_MODEL), row),
            pl.BlockSpec((D_MODEL, D_MODEL), const),
            pl.BlockSpec((1, D_MODEL), const),
            pl.BlockSpec((1, D_MODEL), const),
        ],
        out_specs=pl.BlockSpec((tm, D_MODEL), row),
        out_shape=jax.ShapeDtypeStruct((m, D_MODEL), F32),
        compiler_params=_cparams(1),
        name="outproj_ln",
    )(y, z_arr, x, w, g.reshape(1, D_MODEL), b.reshape(1, D_MODEL))


def _rwkv_prep_kernel(x_ref, win_ref, prev_ref, mu_ref, w0_ref, a0_ref, lora_ref, kk_ref, ka_ref,
                      r_o, w_o, k_o, v_o, kk_o, a_o, z_o, last_o, carry, *, tt, nb, log_decay):
    p = jnp.dot(x_ref[...].astype(BF16), win_ref[...], preferred_element_type=F32)
    seq = tt // nb
    for i in range(nb):
        last_o[i] = p[(i + 1) * seq - 1:(i + 1) * seq, :]
    row = lax.broadcasted_iota(jnp.int32, (tt, 1), 0)
    if nb == 1:
        @pl.when(pl.program_id(1) == 0)
        def _():
            carry[0:1, :] = prev_ref[0]

        before = carry[0:1, :]
        first = row == 0
    else:
        before = jnp.concatenate([jnp.broadcast_to(prev_ref[i], (seq, A_NCOLS)) for i in range(nb)], axis=0)
        first = (row & (seq - 1)) == 0
    p_shift = jnp.where(first, before, pltpu.roll(p, 1, axis=0))
    carry[0:1, :] = p[tt - 1:tt, :]
    pm = p + (p_shift - p) * mu_ref[...]
    r = pm[:, 0:A_WIDTH]
    k = pm[:, A_WIDTH:2 * A_WIDTH]
    v = pm[:, 2 * A_WIDTH:3 * A_WIDTH]
    z = pm[:, 3 * A_WIDTH:4 * A_WIDTH]
    lo = pm[:, 4 * A_WIDTH:A_NCOLS]
    is_w = lax.broadcasted_iota(jnp.int32, (1, 2 * A_LORA), 1) < A_LORA
    lo = jnp.where(is_w, jnp.tanh(lo), lo)
    lora = jnp.dot(lo.astype(BF16), lora_ref[...], preferred_element_type=F32)
    u = w0_ref[...] + lora[:, 0:A_WIDTH]
    w_log = jnp.minimum(u, 0.0) - jnp.log(1.0 + jnp.exp(-jnp.abs(u))) - 0.5
    a = _sigmoid(a0_ref[...] + lora[:, A_WIDTH:2 * A_WIDTH])
    r_o[...] = r
    w_o[...] = -jnp.exp(w_log) if log_decay else jnp.exp(-jnp.exp(w_log))
    k_o[...] = k * (1.0 + (a - 1.0) * ka_ref[...])
    v_o[...] = v
    kk_o[...] = k * kk_ref[...]
    a_o[...] = a
    z_o[...] = z


def _rwkv_prep(x, w_in, p_prev, mu, w0, a0, lora_w, k_k, k_a, batch, seq, tt, log_decay):
    nb = max(1, tt // seq)
    assert (seq % tt == 0 and nb == 1) or (tt % seq == 0 and batch % nb == 0 and seq & (seq - 1) == 0)
    nt = max(1, seq // tt)
    row = lambda b, t: (b * nt + t, 0)
    const = lambda b, t: (0, 0)
    vec = lambda a: a.reshape(1, -1)
    out = jax.ShapeDtypeStruct((batch * seq, A_WIDTH), F32)
    per_seq = pl.BlockSpec((nb, 1, A_NCOLS), lambda b, t: (b, 0, 0))
    return pl.pallas_call(
        functools.partial(_rwkv_prep_kernel, tt=tt, nb=nb, log_decay=log_decay),
        grid=(batch // nb, nt),
        in_specs=[
            pl.BlockSpec((tt, D_MODEL), row),
            pl.BlockSpec((D_MODEL, A_NCOLS), const),
            per_seq,
            pl.BlockSpec((1, A_NCOLS), const),
            pl.BlockSpec((1, A_WIDTH), const),
            pl.BlockSpec((1, A_WIDTH), const),
            pl.BlockSpec((2 * A_LORA, 2 * A_WIDTH), const),
            pl.BlockSpec((1, A_WIDTH), const),
            pl.BlockSpec((1, A_WIDTH), const),
        ],
        out_specs=[pl.BlockSpec((tt, A_WIDTH), row)] * 7 + [per_seq],
        out_shape=[out] * 7 + [jax.ShapeDtypeStruct((batch, 1, A_NCOLS), F32)],
        scratch_shapes=[pltpu.VMEM((8, A_NCOLS), F32)],
        compiler_params=_cparams(2),
        name="rwkv_prep",
    )(x, w_in, p_prev.reshape(batch, 1, A_NCOLS), vec(mu), vec(w0), vec(a0), lora_w, vec(k_k), vec(k_a))


def _split3(x):
    hi = x.astype(BF16)
    rest = x - hi.astype(F32)
    mid = rest.astype(BF16)
    return hi, mid, (rest - mid.astype(F32)).astype(BF16)


def _dot_sel(x, sel):
    hi, mid, lo = _split3(x)
    d = lambda a: jnp.dot(a, sel, preferred_element_type=F32)
    return d(hi) + (d(mid) + d(lo))


def _rwkv_chunk_kernel(r_ref, lw_ref, k_ref, v_ref, kkr_ref, a_ref, s0_ref, rk_ref, lg_ref, lb_ref,
                       y_ref, sout_ref, state, *, ln, bb):
    c = pl.program_id(1)
    half = A_HEAD_DIM
    n_pairs = A_WIDTH // LANES

    @pl.when(c == 0)
    def _():
        zero = jnp.zeros((half, half), F32)
        for b in range(bb):
            for p in range(n_pairs):
                top = jnp.concatenate([s0_ref[b, 2 * p], zero], axis=1)
                bottom = jnp.concatenate([zero, s0_ref[b, 2 * p + 1]], axis=1)
                state[b, p] = jnp.concatenate([top, bottom], axis=0)

    row = lax.broadcasted_iota(jnp.int32, (ln, 1), 0)
    lo = lax.broadcasted_iota(jnp.int32, (1, LANES), 1) < half
    r2 = lax.broadcasted_iota(jnp.int32, (2 * ln, ln), 0) & (ln - 1)
    c2 = lax.broadcasted_iota(jnp.int32, (2 * ln, ln), 1)
    strict2, incl2 = c2 < r2, c2 <= r2
    rr = lax.broadcasted_iota(jnp.int32, (LANES, LANES), 0)
    cc = lax.broadcasted_iota(jnp.int32, (LANES, LANES), 1)
    same_head = (rr & half) == (cc & half)
    seg = jnp.where(same_head, 1.0, 0.0).astype(BF16)
    rb = lax.broadcasted_iota(jnp.int32, (2 * ln, 2 * ln), 0)
    cb = lax.broadcasted_iota(jnp.int32, (2 * ln, 2 * ln), 1)
    strict_bd = ((cb & (ln - 1)) + jnp.where((rb & ln) == (cb & ln), 0, ln)) < (rb & (ln - 1))
    split = lambda x: jnp.concatenate([jnp.where(lo, x, 0.0), jnp.where(lo, 0.0, x)], axis=0)
    merge = lambda x2: jnp.where(lo, x2[0:ln], x2[ln:2 * ln])
    mm = lambda a, b: jnp.dot(a.astype(BF16), b.astype(BF16), preferred_element_type=F32)

    pairs = range(bb * n_pairs)
    sls = [slice((p % n_pairs) * LANES, (p % n_pairs + 1) * LANES) for p in pairs]
    rws = [slice((p // n_pairs) * ln, (p // n_pairs + 1) * ln) for p in pairs]
    load = lambda ref: [ref[rw, sl] for rw, sl in zip(rws, sls)]
    r, lw, k, v, kkr, a = (load(ref) for ref in (r_ref, lw_ref, k_ref, v_ref, kkr_ref, a_ref))
    norm2 = [_dot_sel(x * x, seg) for x in kkr]
    kk = [x * lax.rsqrt(jnp.maximum(n2, 1e-24)) for x, n2 in zip(kkr, norm2)]
    bv = [x * y for x, y in zip(kk, a)]
    cum = lw
    d = 1
    while d < ln:
        cum = [x + jnp.where(row >= d, pltpu.roll(x, d, axis=0), 0.0) for x in cum]
        d *= 2
    tot = [x[ln - 1:ln, :] for x in cum]
    kkd = [x * jnp.exp(cm - l) for x, cm, l in zip(kk, cum, lw)]
    rd = [x * jnp.exp(cm) for x, cm in zip(r, cum)]
    w_inv = [jnp.exp(-cm) for cm in cum]
    kd = [x * w for x, w in zip(k, w_inv)]
    bd = [x * w for x, w in zip(bv, w_inv)]
    s_old = [state[p // n_pairs, p % n_pairs] for p in pairs]

    xk = [split(x).astype(BF16) for x in kkd]
    xr = [split(x).astype(BF16) for x in rd]
    nil = [jnp.where(strict_bd, -_dot_t(x, split(y).astype(BF16)), 0.0) for x, y in zip(xk, bd)]
    gk = [_dot_t(jnp.concatenate([x, y], axis=0), z.astype(BF16)) for x, y, z in zip(xk, xr, kd)]
    sx = [_dot_t(jnp.concatenate([x, y], axis=0).astype(BF16), s.astype(BF16)) for x, y, s in zip(kkd, rd, s_old)]
    av = [mm(jnp.concatenate([jnp.where(strict2, g[0:2 * ln], 0.0), jnp.where(incl2, g[2 * ln:4 * ln], 0.0)], axis=0), x)
          for g, x in zip(gk, v)]
    u2 = [split(s[0:ln] + merge(x[0:2 * ln])) for s, x in zip(sx, av)]
    power = nil
    step = 1
    while step < ln:
        u2 = [x + mm(pw, x) for x, pw in zip(u2, power)]
        step *= 2
        if step < ln:
            power = [mm(pw, pw) for pw in power]
    u = [x[0:ln] + x[ln:2 * ln] for x in u2]
    a_rb = [jnp.where(incl2, _dot_t(x, y.astype(BF16)), 0.0) for x, y in zip(xr, bd)]
    y = [s[ln:2 * ln] + merge(x[2 * ln:4 * ln]) - merge(mm(g, w)) for s, x, g, w in zip(sx, av, a_rb, u)]

    inv_n = 1.0 / half
    for p in pairs:
        sl = sls[p]
        w_rest = jnp.exp(tot[p] - cum[p])
        upd = mm(jnp.concatenate([v[p], u[p]], axis=0).T,
                 jnp.concatenate([k[p] * w_rest, -(bv[p] * w_rest)], axis=0))
        state[p // n_pairs, p % n_pairs] = s_old[p] * jnp.exp(tot[p]) + jnp.where(same_head, upd, 0.0)
        mean = _dot_sel(y[p], seg) * inv_n
        cen = y[p] - mean
        var = _dot_sel(cen * cen, seg) * inv_n
        bonus = _dot_sel(r[p] * k[p] * rk_ref[:, sl], seg)
        y_ref[rws[p], sl] = cen * lax.rsqrt(var + A_GN_EPS) * lg_ref[:, sl] + lb_ref[:, sl] + bonus * v[p]

    @pl.when(c == pl.num_programs(1) - 1)
    def _():
        for b in range(bb):
            for p in range(n_pairs):
                sout_ref[b, 2 * p] = state[b, p, 0:half, 0:half]
                sout_ref[b, 2 * p + 1] = state[b, p, half:LANES, half:LANES]


def _rwkv_chunked(r, lw, k, v, kkr, a, s0, r_k, lnx_g, lnx_b, batch, seq, ln):
    nc = seq // ln
    n_pairs = A_WIDTH // LANES
    bb = RWKV_ROWS_PER_STEP if nc == 1 and batch % RWKV_ROWS_PER_STEP == 0 else 1
    row = pl.BlockSpec((bb * ln, A_WIDTH), lambda b, c: (b * nc + c, 0))
    st = pl.BlockSpec((bb, A_HEADS, A_HEAD_DIM, A_HEAD_DIM), lambda b, c: (b, 0, 0, 0))
    par = pl.BlockSpec((1, A_WIDTH), lambda b, c: (0, 0))
    return pl.pallas_call(
        functools.partial(_rwkv_chunk_kernel, ln=ln, bb=bb),
        grid=(batch // bb, nc),
        in_specs=[row] * 6 + [st] + [par] * 3,
        out_specs=[row, st],
        out_shape=[jax.ShapeDtypeStruct((batch * seq, A_WIDTH), F32),
                   jax.ShapeDtypeStruct((batch, A_HEADS, A_HEAD_DIM, A_HEAD_DIM), F32)],
        scratch_shapes=[pltpu.VMEM((bb, n_pairs, LANES, LANES), F32)],
        compiler_params=_cparams(2),
        name="rwkv_chunk",
    )(r, lw, k, v, kkr, a, s0, r_k.reshape(1, A_WIDTH), lnx_g.reshape(1, A_WIDTH), lnx_b.reshape(1, A_WIDTH))


def _rwkv_layer(x, batch, seq, s0, p_prev, prm, ln_g, ln_b, tm, tt_prep):
    (w_in, mu, w0, w2, a0, a2, k_k, k_a, r_k, lnx_g, lnx_b, w_out) = prm
    ln = min(RWKV_CHUNK, seq)
    assert seq % ln == 0 and ln % 8 == 0 and ln & (ln - 1) == 0
    zeros = jnp.zeros((A_LORA, A_WIDTH), F32)
    lora_w = jnp.concatenate([jnp.concatenate([w2, zeros], axis=1), jnp.concatenate([zeros, a2], axis=1)], axis=0)
    r, w, k, v, kkr, a, z, p_last = _rwkv_prep(x, w_in.astype(BF16), p_prev, mu, w0, a0, lora_w.astype(BF16), k_k,
                                               k_a, batch, seq, tt_prep, True)
    y, s_final = _rwkv_chunked(r, w, k, v, kkr, a, s0, r_k, lnx_g, lnx_b, batch, seq, ln)
    x_new = _outproj_ln(y, z, 0, x, w_out.astype(BF16), ln_g, ln_b, tm)
    return x_new, s_final, p_last.reshape(batch, A_NCOLS)


def _pool_kernel(u_ref, buf_ref, wg_ref, scale_ref, y_ref, nbuf_ref, ext, *, tt, front_valid):
    t = pl.program_id(1)

    @pl.when(t == 0)
    def _():
        ext[1:B_CARRY, :] = buf_ref[0]

    @pl.when(t > 0)
    def _():
        ext[0:B_CARRY, :] = ext[tt:tt + B_CARRY, :]

    u = u_ref[...]
    ext[B_CARRY:B_CARRY + tt, :] = u
    pos = t * tt + lax.broadcasted_iota(jnp.int32, (tt, 1), 0)
    for gi, win in enumerate(B_WINDOWS):
        lo, hi = gi * B_GROUP_W, (gi + 1) * B_GROUP_W
        ug = u[:, lo:hi]
        acc = ug
        for s in range(1, win):
            acc = acc + ext[B_CARRY - s:B_CARRY - s + tt, lo:hi]
        if front_valid:
            cnt = float(win)
        else:
            cnt = jnp.minimum(pos + 1, win).astype(F32)
        d = acc / cnt - ug
        yg = jnp.dot(d.astype(BF16), wg_ref[gi], preferred_element_type=F32)
        y_ref[:, lo:hi] = yg * scale_ref[:, lo:hi]

    @pl.when(t == pl.num_programs(1) - 1)
    def _():
        nbuf_ref[0] = ext[tt + 1:tt + B_CARRY, :]


def _pool(uz, buf, w_grp, scale, batch, seq, tt, front_valid):
    nt = seq // tt
    return pl.pallas_call(
        functools.partial(_pool_kernel, tt=tt, front_valid=front_valid),
        grid=(batch, nt),
        in_specs=[
            pl.BlockSpec((tt, B_WIDTH), lambda b, t: (b * nt + t, 0)),
            pl.BlockSpec((1, B_BUF, B_WIDTH), lambda b, t: (b, 0, 0)),
            pl.BlockSpec((len(B_WINDOWS), B_GROUP_W, B_GROUP_W), lambda b, t: (0, 0, 0)),
            pl.BlockSpec((1, B_WIDTH), lambda b, t: (0, 0)),
        ],
        out_specs=[
            pl.BlockSpec((tt, B_WIDTH), lambda b, t: (b * nt + t, 0)),
            pl.BlockSpec((1, B_BUF, B_WIDTH), lambda b, t: (b, 0, 0)),
        ],
        out_shape=[jax.ShapeDtypeStruct((batch * seq, B_WIDTH), F32),
                   jax.ShapeDtypeStruct((batch, B_BUF, B_WIDTH), F32)],
        scratch_shapes=[pltpu.VMEM((tt + B_CARRY, B_WIDTH), F32)],
        compiler_params=_cparams(2),
        name="pool",
    )(uz, buf, w_grp, scale.reshape(1, B_WIDTH))


def _pool_layer(x, batch, seq, buf, front_valid, prm, ln_g, ln_b, tm, tt):
    w_in, w_grp, scale, w_out = prm
    uz = _proj(x, w_in.astype(BF16), tm)
    y, new_buf = _pool(uz, buf, w_grp.astype(BF16), scale, batch, seq, tt, front_valid)
    x_new = _outproj_ln(y, uz, 1, x, w_out.astype(BF16), ln_g, ln_b, tm)
    return x_new, new_buf


NEG = -1e30
C_SCALE = C_HEAD_DIM ** -0.5


def _alibi_slopes():
    return jnp.power(2.0, -8.0 * (jnp.arange(C_HEADS, dtype=F32) + 1.0) / C_HEADS)


def _dot_t(a, b):
    return lax.dot_general(a, b, (((1,), (1,)), ((), ())), preferred_element_type=F32)


def _masked_softmax_parts(scores, masks, axis=1):
    top = None
    for s, k in zip(scores, masks):
        part = jnp.where(k, s, -jnp.inf)
        top = part if top is None else jnp.maximum(top, part)
    m = jnp.max(top, axis=axis, keepdims=True)
    m = jnp.where(jnp.isfinite(m), m, 0.0)
    es = [jnp.where(k, jnp.exp(s - m), 0.0) for s, k in zip(scores, masks)]
    total = es[0]
    for e in es[1:]:
        total = total + e
    den = jnp.maximum(jnp.sum(total, axis=axis, keepdims=True), 1e-30)
    return [e / den for e in es]


def _top_n_mask(imp, top_n):
    n = imp.shape[1]
    idx = lax.broadcasted_iota(jnp.int32, (1, n), 1)
    rank = jnp.zeros(imp.shape, F32)
    for c in range(n):
        col = imp[:, c:c + 1]
        rank = rank + jnp.where(idx > c, jnp.where(col >= imp, 1.0, 0.0), jnp.where(col > imp, 1.0, 0.0))
    return jnp.where(rank < top_n, jnp.where(imp >= 0.0, 1.0, 0.0), 0.0)


def _top_n_mask_wide(imp, top_n):
    n = imp.shape[1]
    shift = n.bit_length() - 1
    assert n == 1 << shift
    lane = lax.broadcasted_iota(jnp.int32, (n, n * n), 1)
    src = lax.broadcasted_iota(jnp.int32, (n, n * n), 0)
    from_cand = jnp.where(src == lax.shift_right_logical(lane, shift), 1.0, 0.0).astype(BF16)
    from_entry = jnp.where(src == (lane & (n - 1)), 1.0, 0.0).astype(BF16)
    cand = _dot_sel(imp, from_cand)
    entry = _dot_sel(imp, from_entry)
    c_idx = lax.shift_right_logical(lane[0:1], shift)
    s_idx = lane[0:1] & (n - 1)
    beats = jnp.where(s_idx > c_idx, jnp.where(cand >= entry, 1.0, 0.0), jnp.where(cand > entry, 1.0, 0.0))
    rank = _dot_t(beats.astype(BF16), from_entry)
    return jnp.where(rank < top_n, jnp.where(imp >= 0.0, 1.0, 0.0), 0.0)


def _block_expand(n_blocks, kpos):
    blk = lax.broadcasted_iota(jnp.int32, (n_blocks, kpos.shape[1]), 0)
    return jnp.where(lax.shift_right_logical(kpos, 6) == blk, 1.0, 0.0).astype(BF16)


def _cmp_kernel(kv_ref, w_ref, e_ref, o_ref):
    x = kv_ref[...]
    n_pair = x.shape[0] // (2 * C_CMP_BLOCK)
    x4 = x.reshape(n_pair, 2, C_CMP_BLOCK, x.shape[1])
    w = w_ref[...][None]
    e_ref[0] = jnp.sum(x4[:, 0] * w, axis=1)
    o_ref[0] = jnp.sum(x4[:, 1] * w, axis=1)


def _cmp(p, col_block, w, batch, seq):
    width = w.shape[1]
    n_pair = seq // (2 * C_CMP_BLOCK)
    out = jax.ShapeDtypeStruct((batch, n_pair, width), F32)
    return pl.pallas_call(
        _cmp_kernel,
        grid=(batch,),
        in_specs=[pl.BlockSpec((seq, width), lambda b: (b, col_block)), pl.BlockSpec(w.shape, lambda b: (0, 0))],
        out_specs=[pl.BlockSpec((1, n_pair, width), lambda b: (b, 0, 0))] * 2,
        out_shape=[out, out],
        compiler_params=_cparams(1),
        name="nsa_cmp",
    )(p, w)


def _nsa_prompt_kernel(q_ref, g_ref, ce_ref, co_ref, sel_ref, win_ref, slope_ref, o_ref, *, tq, tk):
    qi = pl.program_id(2)
    q0 = qi * tq
    hd = C_HEAD_DIM
    rows = C_HPG * tq
    n_pair = ce_ref.shape[1]
    stack = lambda f: jnp.concatenate([f(j) for j in range(C_HPG)], axis=0)
    tile4 = lambda x: jnp.concatenate([x] * C_HPG, axis=0)
    lo = lax.broadcasted_iota(jnp.int32, (1, LANES), 1) < hd
    q_all = q_ref[...]

    def q_head(j):
        pair_tile = q_all[:, (j // 2) * LANES:(j // 2 + 1) * LANES]
        return jnp.where(lo, pair_tile if j % 2 == 0 else pltpu.roll(pair_tile, hd, axis=1), 0.0)

    qb = (stack(q_head) * C_SCALE).astype(BF16)
    slope = stack(lambda j: jnp.broadcast_to(slope_ref[0, j:j + 1, 0:1], (tq, 1)))
    slope_keys = jnp.broadcast_to(slope, (rows, tk))
    qpos_t = q0 + lax.broadcasted_iota(jnp.int32, (tq, 1), 0)
    qpos = tile4(qpos_t)

    n_gate = 3 * C_HPG
    src = lax.broadcasted_iota(jnp.int32, (LANES, n_gate * LANES), 0)
    dst = lax.shift_right_logical(lax.broadcasted_iota(jnp.int32, (LANES, n_gate * LANES), 1), 7)
    gates = _dot_sel(_sigmoid(g_ref[...]), jnp.where(src == dst, 1.0, 0.0).astype(BF16))
    gate = lambda br: stack(lambda j: gates[:, (3 * j + br) * LANES:(3 * j + br + 1) * LANES])

    pair = lax.broadcasted_iota(jnp.int32, (1, n_pair), 1)
    ce, co = ce_ref[0].astype(BF16), co_ref[0].astype(BF16)
    scores, masks = [], []
    for par, c in ((0, ce), (1, co)):
        dist = qpos - ((2 * pair + par + 1) * C_CMP_BLOCK - 1)
        scores.append(_dot_t(qb, c) - slope * dist.astype(F32))
        masks.append(dist >= 0)
    p_e, p_o = _masked_softmax_parts(scores, masks)
    o_c = (jnp.dot(p_e.astype(BF16), ce, preferred_element_type=F32)
           + jnp.dot(p_o.astype(BF16), co, preferred_element_type=F32))
    imp_h = p_e + p_o
    imp = imp_h[0:tq]
    for j in range(1, C_HPG):
        imp = imp + imp_h[j * tq:(j + 1) * tq]
    cur = lax.shift_right_logical(qpos_t, 6)
    imp = jnp.where(pair == cur, C_FORCE, imp)
    imp = jnp.where(pair <= cur, imp, -1.0)
    sel = _top_n_mask_wide(imp, C_TOP_N).astype(BF16)

    def attend(carry, k0, kv, penalty_of):
        m, acc = carry
        kpos = k0 + lax.broadcasted_iota(jnp.int32, (1, tk), 1)
        dist = qpos_t - kpos
        s = (_dot_t(qb, kv.astype(BF16)) - slope_keys * tile4(dist.astype(F32))) + tile4(penalty_of(dist, kpos))
        m_new = jnp.maximum(m, jnp.max(s, axis=1, keepdims=True))
        p = jnp.exp(s - m_new).astype(BF16)
        ones_v = jnp.where(lo, 1.0, kv).astype(BF16)
        acc = jnp.exp(m - m_new) * acc + jnp.dot(p, ones_v, preferred_element_type=F32)
        return m_new, acc

    def finish(carry):
        _, acc = carry
        return acc / jnp.where(lo, 1.0, jnp.maximum(pltpu.roll(acc, hd, axis=1), 1e-30))

    init = (jnp.full((rows, 1), NEG, F32), jnp.zeros((rows, LANES), F32))

    def sel_step(c, carry):
        k0 = pl.multiple_of(c * tk, tk)

        def penalty(dist, kpos):
            chosen = jnp.dot(sel, _block_expand(n_pair, kpos), preferred_element_type=F32)
            return jnp.where(dist >= 0, jnp.where(chosen > 0.5, 0.0, NEG), NEG)

        return attend(carry, k0, sel_ref[pl.ds(k0, tk), :], penalty)

    o_s = finish(lax.fori_loop(0, qi, sel_step, sel_step(qi, init)))

    w0 = jnp.maximum(q0 - C_WINDOW, 0)
    n_win = (tq + C_WINDOW) // tk
    own = (q0 - w0) // tk

    def win_step(i, carry):
        k0 = pl.multiple_of(w0 + lax.rem(own + i, n_win) * tk, tk)
        penalty = lambda dist, kpos: jnp.where(dist >= 0, jnp.where(dist < C_WINDOW, 0.0, NEG), NEG)
        return attend(carry, k0, win_ref[pl.ds(k0, tk), :], penalty)

    o_w = finish(lax.fori_loop(1, n_win, win_step, win_step(0, init)))

    o = gate(0) * o_c + gate(1) * o_s + gate(2) * o_w
    head = lambda j: o[j * tq:(j + 1) * tq]
    o_ref[...] = jnp.concatenate([jnp.where(lo, pltpu.roll(head(2 * i), hd, axis=1), head(2 * i + 1))
                                  for i in range(C_HPG // 2)], axis=1)


def _nsa_prompt_layer(x, batch, seq, prm, ln_g, ln_b):
    w_in, cmp_wk, cmp_wv, w_out = prm
    hd, grp = C_HEAD_DIM, C_KV_HEADS
    tq = tk = 256
    assert seq % tq == 0 and seq >= tq + C_WINDOW and C_WINDOW % tk == 0 and seq % (2 * C_CMP_BLOCK) == 0
    assert tq == tk and seq == 2 * C_CMP_BLOCK * (seq // C_SEL_BLOCK)
    kv_w = lambda i: w_in[:, C_WIDTH + C_KV_WIDTH * i:C_WIDTH + C_KV_WIDTH * (i + 1)].reshape(D_MODEL, grp, hd)
    kv_pair = lambda a, b: jnp.concatenate([kv_w(a), kv_w(b)], axis=2).reshape(D_MODEL, grp * 2 * hd)
    g_lo = C_WIDTH + 6 * C_KV_WIDTH
    g_w = w_in[:, g_lo:g_lo + 3 * C_HEADS].reshape(D_MODEL, grp, 3 * C_HPG)
    g_w = jnp.pad(g_w, ((0, 0), (0, 0), (0, LANES - 3 * C_HPG))).reshape(D_MODEL, grp * LANES)
    w_p = jnp.concatenate([w_in[:, :C_WIDTH], w_in[:, g_lo + 3 * C_HEADS:], kv_pair(0, 1), kv_pair(2, 3),
                           kv_pair(4, 5), g_w, w_in[:, C_WIDTH:C_WIDTH + 4 * C_KV_WIDTH]], axis=1)
    n_main = 2 * C_WIDTH + 4 * grp * 2 * hd
    p, *new_rows = _proj(x, w_p.astype(BF16), 256, (n_main,) + (C_KV_WIDTH,) * 4)
    kv_lo = 2 * C_WIDTH
    kvw = grp * 2 * hd
    cw = jnp.concatenate([jnp.broadcast_to(cmp_wk[:, None], (C_CMP_BLOCK, hd)),
                          jnp.broadcast_to(cmp_wv[:, None], (C_CMP_BLOCK, hd))], axis=1)
    ce, co = _cmp(p, kv_lo // kvw, jnp.tile(cw, (1, grp)), batch, seq)
    slopes = jnp.broadcast_to(jnp.pad(_alibi_slopes().reshape(grp, C_HPG), ((0, 0), (0, 8 - C_HPG)))[:, :, None],
                              (grp, 8, LANES))
    nq = seq // tq
    n_pair = seq // (2 * C_CMP_BLOCK)
    gw = 2 * hd
    o = pl.pallas_call(
        functools.partial(_nsa_prompt_kernel, tq=tq, tk=tk),
        grid=(batch, grp, nq),
        in_specs=[
            pl.BlockSpec((tq, C_HPG * hd), lambda b, g, i: (b * nq + i, g)),
            pl.BlockSpec((tq, LANES), lambda b, g, i: (b * nq + i, (kv_lo + 3 * kvw) // LANES + g)),
            pl.BlockSpec((1, n_pair, gw), lambda b, g, i: (b, 0, g)),
            pl.BlockSpec((1, n_pair, gw), lambda b, g, i: (b, 0, g)),
            pl.BlockSpec((seq, gw), lambda b, g, i: (b, (kv_lo + kvw) // gw + g)),
            pl.BlockSpec((seq, gw), lambda b, g, i: (b, (kv_lo + 2 * kvw) // gw + g)),
            pl.BlockSpec((1, 8, LANES), lambda b, g, i: (g, 0, 0)),
        ],
        out_specs=pl.BlockSpec((tq, C_HPG * hd), lambda b, g, i: (b * nq + i, g)),
        out_shape=jax.ShapeDtypeStruct((batch * seq, C_WIDTH), F32),
        compiler_params=_cparams(3),
        name="nsa_prompt",
    )(p, p, ce, co, p, p, slopes)
    x_new = _outproj_ln(o, p, 1, x, w_out.astype(BF16), ln_g, ln_b, 512)
    rows = tuple(r.reshape(batch, seq, grp, hd) for r in new_rows)
    keep = min(C_WINDOW, seq)
    win = p.reshape(batch, seq, -1)[:, seq - keep:, kv_lo + 2 * kvw:kv_lo + 3 * kvw].reshape(batch, keep, grp, 2, hd)
    return x_new, rows, win[:, :, :, 0], win[:, :, :, 1]


def _nsa_sample_kernel(tbl_ref, *refs, n_pages, ts):
    del tbl_ref
    ck, cv, sk, sv = (refs[i * n_pages:(i + 1) * n_pages] for i in range(4))
    (q_ref, g_ref, ksn_ref, vsn_ref, kwn_ref, vwn_ref, wk_ref, wv_ref, cwk_ref, cwv_ref, slope_ref,
     o_ref, ke_s, ko_s, ve_s, vo_s, new_s) = refs[4 * n_pages:]
    hd, grp = C_HEAD_DIM, C_KV_HEADS
    rows = C_HEADS * ts
    past = n_pages * PAGE_SIZE
    n_pair = past // C_SEL_BLOCK
    qpos = past + lax.rem(lax.broadcasted_iota(jnp.int32, (rows, 1), 0), ts)
    qb = (q_ref[0] * C_SCALE).astype(BF16)
    slope = slope_ref[:, 0:1]

    per_page = PAGE_SIZE // C_CMP_BLOCK
    for pages, cw_ref, e_s, o_s in ((ck, cwk_ref, ke_s, ko_s), (cv, cwv_ref, ve_s, vo_s)):
        cw = cw_ref[...][None]
        for pg in range(n_pages):
            tok = jnp.sum(pages[pg][0].reshape(per_page, C_CMP_BLOCK, grp * hd) * cw, axis=1)
            for i in range(per_page):
                n = pg * per_page + i
                dst = e_s if n % 2 == 0 else o_s
                dst[n // 2:n // 2 + 1, :] = tok[i:i + 1]

    pair = lax.broadcasted_iota(jnp.int32, (1, n_pair), 1)
    scores, masks = [], []
    for par, k_s in ((0, ke_s), (1, ko_s)):
        dist = qpos - ((2 * pair + par + 1) * C_CMP_BLOCK - 1)
        scores.append(_dot_t(qb, k_s[...].astype(BF16)) - slope * dist.astype(F32))
        masks.append(dist >= 0)
    p_e, p_o = _masked_softmax_parts(scores, masks)
    o_c = (jnp.dot(p_e.astype(BF16), ve_s[...].astype(BF16), preferred_element_type=F32)
           + jnp.dot(p_o.astype(BF16), vo_s[...].astype(BF16), preferred_element_type=F32))
    imp_h = p_e + p_o
    sel_rows = []
    for g in range(grp):
        base = g * C_HPG * ts
        imp = imp_h[base:base + ts]
        for j in range(1, C_HPG):
            imp = imp + imp_h[base + j * ts:base + (j + 1) * ts]
        sel_g = _top_n_mask(imp, C_TOP_N - 1)
        sel_rows += [sel_g] * C_HPG
    sel_rows = jnp.concatenate(sel_rows, axis=0).astype(BF16)

    lane_pos = lax.broadcasted_iota(jnp.int32, (1, PAGE_SIZE), 1)

    def new_rows(slot, k_ref, v_ref):
        new_s[2 * slot:2 * slot + 2] = jnp.zeros((2,) + new_s.shape[1:], F32)
        new_s[2 * slot, 0:ts, :] = k_ref[...]
        new_s[2 * slot + 1, 0:ts, :] = v_ref[...]
        return new_s[2 * slot].astype(BF16), new_s[2 * slot + 1].astype(BF16)

    def softmax_attend(tiles):
        scores = []
        for kb, _, kpos, pen in tiles:
            dist = qpos - kpos
            scores.append((_dot_t(qb, kb) - slope * dist.astype(F32)) + pen(dist))
        top = scores[0]
        for s in scores[1:]:
            top = jnp.maximum(top, s)
        m = jnp.max(top, axis=1, keepdims=True)
        probs = [jnp.exp(s - m) for s in scores]
        total = probs[0]
        for p in probs[1:]:
            total = total + p
        acc = None
        for p, (_, vb, _, _) in zip(probs, tiles):
            part = jnp.dot(p.astype(BF16), vb, preferred_element_type=F32)
            acc = part if acc is None else acc + part
        return acc / jnp.maximum(jnp.sum(total, axis=1, keepdims=True), 1e-30)

    causal = lambda dist: jnp.where(dist >= 0, 0.0, NEG)
    window = lambda dist: jnp.where(dist >= 0, jnp.where(dist < C_WINDOW, 0.0, NEG), NEG)

    tiles = []
    for pg in range(n_pages):
        kpos = pg * PAGE_SIZE + lane_pos
        chosen = jnp.dot(sel_rows, _block_expand(n_pair, kpos), preferred_element_type=F32)
        pen = lambda dist, chosen=chosen: jnp.where(dist >= 0, jnp.where(chosen > 0.5, 0.0, NEG), NEG)
        tiles.append((sk[pg][0].astype(BF16), sv[pg][0].astype(BF16), kpos, pen))
    tiles.append(new_rows(0, ksn_ref, vsn_ref) + (past + lane_pos, causal))
    o_s = softmax_attend(tiles)

    n_win = wk_ref.shape[1]
    tiles = []
    for c in range(n_win // PAGE_SIZE):
        lo = c * PAGE_SIZE
        tiles.append((wk_ref[0, lo:lo + PAGE_SIZE, :].astype(BF16), wv_ref[0, lo:lo + PAGE_SIZE, :].astype(BF16),
                      past - n_win + lo + lane_pos, window))
    tiles.append(new_rows(1, kwn_ref, vwn_ref) + (past + lane_pos, window))
    o_w = softmax_attend(tiles)

    per_group = C_HPG * ts
    diag = lambda full: jnp.concatenate(
        [full[g * per_group:(g + 1) * per_group, g * hd:(g + 1) * hd] for g in range(grp)], axis=0)
    gates = _sigmoid(g_ref[0])
    o_ref[0] = gates[:, 0:1] * diag(o_c) + gates[:, 1:2] * diag(o_s) + gates[:, 2:3] * diag(o_w)


def _nsa_sample_layer(x, batch, ts, caches, page_table, win_k, win_v, prm, ln_g, ln_b):
    w_in, cmp_wk, cmp_wv, w_out = prm
    hd, grp = C_HEAD_DIM, C_KV_HEADS
    n_pages = page_table.shape[1]
    n_win = win_k.shape[1]
    past = n_pages * PAGE_SIZE
    rows = C_HEADS * ts
    assert past % C_SEL_BLOCK == 0 and ts <= C_SEL_BLOCK and ts % 8 == 0 and n_win == C_WINDOW and past >= n_win
    assert ts & (ts - 1) == 0 and ts <= 16 and rows == LANES
    g_lo = C_WIDTH + 6 * C_KV_WIDTH
    w_s = jnp.concatenate([w_in[:, :C_WIDTH], w_in[:, g_lo + 3 * C_HEADS:], w_in[:, C_WIDTH:g_lo],
                           jnp.pad(w_in[:, g_lo:g_lo + 3 * C_HEADS], ((0, 0), (0, LANES - 3 * C_HEADS)))], axis=1)
    p = _proj(x, w_s.astype(BF16), 512)
    kv_lo = 2 * C_WIDTH
    q = p[:, :C_WIDTH].reshape(batch, ts, grp, C_HPG, hd).transpose(0, 2, 3, 1, 4).reshape(batch, grp, C_HPG * ts, hd)
    q_bd = (q[:, :, :, None, :] * jnp.eye(grp, dtype=F32)[None, :, None, :, None]).reshape(batch, rows, grp * hd)
    g_lo_p = kv_lo + 6 * C_KV_WIDTH
    g_t = p[:, g_lo_p:g_lo_p + 3 * C_HEADS].reshape(batch, ts, C_HEADS, 3).transpose(0, 2, 1, 3).reshape(batch, rows, 3)
    slope_rows = jnp.broadcast_to(jnp.repeat(_alibi_slopes(), ts)[:, None], (rows, LANES))
    bcast = lambda w: jnp.broadcast_to(w[:, None], (C_CMP_BLOCK, grp * hd))
    pools = [c.reshape(c.shape[0], PAGE_SIZE, grp * hd) for c in caches]
    page_spec = lambda pg: pl.BlockSpec((1, PAGE_SIZE, grp * hd), lambda b, tbl: (tbl[b, pg], 0, 0))
    new_spec = lambda i: pl.BlockSpec((ts, C_KV_WIDTH), lambda b, tbl: (b, kv_lo // C_KV_WIDTH + i))
    const2 = lambda shape: pl.BlockSpec(shape, lambda b, tbl: (0, 0))
    per_b = lambda shape: pl.BlockSpec((1,) + shape, lambda b, tbl: (b, 0, 0))
    in_specs = ([page_spec(pg) for _ in range(4) for pg in range(n_pages)]
                + [per_b((rows, grp * hd)), per_b((rows, 3)), new_spec(2), new_spec(3), new_spec(4), new_spec(5),
                   per_b((n_win, grp * hd)), per_b((n_win, grp * hd)),
                   const2((C_CMP_BLOCK, grp * hd)), const2((C_CMP_BLOCK, grp * hd)), const2((rows, LANES))])
    n_pair = past // C_SEL_BLOCK
    o = pl.pallas_call(
        functools.partial(_nsa_sample_kernel, n_pages=n_pages, ts=ts),
        grid_spec=pltpu.PrefetchScalarGridSpec(
            num_scalar_prefetch=1,
            grid=(batch,),
            in_specs=in_specs,
            out_specs=per_b((rows, hd)),
            scratch_shapes=[pltpu.VMEM((n_pair, grp * hd), F32)] * 4 + [pltpu.VMEM((4, PAGE_SIZE, grp * hd), F32)],
        ),
        out_shape=jax.ShapeDtypeStruct((batch, rows, hd), F32),
        compiler_params=_cparams(1),
        name="nsa_sample",
    )(page_table, *[pool for pool in pools for _ in range(n_pages)], q_bd, g_t, p, p, p, p,
      win_k.reshape(batch, n_win, grp * hd), win_v.reshape(batch, n_win, grp * hd), bcast(cmp_wk), bcast(cmp_wv),
      slope_rows)
    new = lambda i: p[:, kv_lo + i * C_KV_WIDTH:kv_lo + (i + 1) * C_KV_WIDTH].reshape(batch, ts, grp, hd)
    y = o.reshape(batch, C_HEADS, ts, hd).transpose(0, 2, 1, 3).reshape(batch * ts, C_WIDTH)
    x_new = _outproj_ln(y, p, 1, x, w_out.astype(BF16), ln_g, ln_b, 512)
    keep = lambda buf, i: jnp.concatenate([buf, new(i)], axis=1)[:, -n_win:]
    return x_new, (new(0), new(1), new(2), new(3)), keep(win_k, 4), keep(win_v, 5)


def kernel(x_prompt, x_sample, state_rwkv_S, state_rwkv_shift, state_pool, cache_cmp_k, cache_cmp_v, cache_sel_k,
           cache_sel_v, state_win_k, state_win_v, page_table, ln_g, ln_b, a_w_in, a_mu, a_w0, a_w2, a_a0, a_a2, a_k_k,
           a_k_a, a_r_k, a_lnx_g, a_lnx_b, a_w_out, b_w_in, b_w_grp, b_scale, b_w_out, c_w_in, c_cmp_wk, c_cmp_wv,
           c_w_out):
    bp, tp, _ = x_prompt.shape
    bs, ts, _ = x_sample.shape
    xp = x_prompt.reshape(bp * tp, D_MODEL)
    xs = x_sample.reshape(bs * ts, D_MODEL)
    s_p, s_s, sh_p, sh_s, pl_p, pl_s = [], [], [], [], [], []
    rows_p, rows_s, wk_p, wk_s, wv_p, wv_s = [], [], [], [], [], []
    for layer in range(DEPTH):
        kind, li = layer % 3, layer // 3
        g, b = ln_g[layer], ln_b[layer]
        if kind == 0:
            prm = (a_w_in[li], a_mu[li], a_w0[li], a_w2[li], a_a0[li], a_a2[li], a_k_k[li], a_k_a[li], a_r_k[li],
                   a_lnx_g[li], a_lnx_b[li], a_w_out[li])
            xp, s_new, sh_new = _rwkv_layer(xp, bp, tp, jnp.zeros((bp, A_HEADS, A_HEAD_DIM, A_HEAD_DIM), F32),
                                            jnp.zeros((bp, A_NCOLS), F32), prm, g, b, 512, 256)
            s_p.append(s_new)
            sh_p.append(sh_new)
            xs, s_new, sh_new = _rwkv_layer(xs, bs, ts, state_rwkv_S[li], state_rwkv_shift[li], prm, g, b, 512, 64)
            s_s.append(s_new)
            sh_s.append(sh_new)
        elif kind == 1:
            prm = (b_w_in[li], b_w_grp[li], b_scale[li], b_w_out[li])
            xp, buf_new = _pool_layer(xp, bp, tp, jnp.zeros((bp, B_BUF, B_WIDTH), F32), False, prm, g, b, 512, 512)
            pl_p.append(buf_new)
            xs, buf_new = _pool_layer(xs, bs, ts, state_pool[li], True, prm, g, b, 512, ts)
            pl_s.append(buf_new)
        else:
            prm = (c_w_in[li], c_cmp_wk[li], c_cmp_wv[li], c_w_out[li])
            xp, rows, wk, wv = _nsa_prompt_layer(xp, bp, tp, prm, g, b)
            rows_p.append(rows)
            wk_p.append(wk)
            wv_p.append(wv)
            caches = (cache_cmp_k[li], cache_cmp_v[li], cache_sel_k[li], cache_sel_v[li])
            xs, rows, wk, wv = _nsa_sample_layer(xs, bs, ts, caches, page_table, state_win_k[li], state_win_v[li],
                                                 prm, g, b)
            rows_s.append(rows)
            wk_s.append(wk)
            wv_s.append(wv)
    stack = jnp.stack
    return (xp.reshape(bp, tp, D_MODEL), xs.reshape(bs, ts, D_MODEL), stack(s_p), stack(s_s), stack(sh_p), stack(sh_s),
            stack(pl_p), stack(pl_s),
            stack([r[0] for r in rows_p]), stack([r[0] for r in rows_s]),
            stack([r[1] for r in rows_p]), stack([r[1] for r in rows_s]),
            stack([r[2] for r in rows_p]), stack([r[2] for r in rows_s]),
            stack([r[3] for r in rows_p]), stack([r[3] for r in rows_s]),
            stack(wk_p), stack(wk_s), stack(wv_p), stack(wv_s))
```

```python
import functools

import jax
import jax.numpy as jnp
import numpy as np
from jax import lax
from jax.experimental import pallas as pl
from jax.experimental.pallas import tpu as pltpu

F32 = jnp.float32
BF16 = jnp.bfloat16

D_MODEL = 1024
DEPTH = 4
DEEPNORM_ALPHA = (2.0 * DEPTH) ** 0.25
LN_EPS = 1e-5

A_HEADS = 16
A_HEAD_DIM = 64
A_WIDTH = 1024
A_LORA = 64
A_NCOLS = 4 * A_WIDTH + 2 * A_LORA
A_GN_EPS = 64e-5
RWKV_CHUNK = 64
RWKV_ROWS_PER_STEP = 4

B_WIDTH = 1024
B_GROUP_W = 256
B_WINDOWS = (2, 4, 8, 16)
B_BUF = 15
B_CARRY = 16

C_HEADS = 16
C_KV_HEADS = 4
C_HPG = 4
C_HEAD_DIM = 64
C_WIDTH = 1024
C_KV_WIDTH = 256
C_CMP_BLOCK = 32
C_SEL_BLOCK = 64
C_TOP_N = 16
C_WINDOW = 512
C_FORCE = 1e9
PAGE_SIZE = 128

LANES = 128
VMEM_LIMIT = 56 * 1024 * 1024


def _cparams(n_axes):
    return pltpu.CompilerParams(dimension_semantics=("arbitrary",) * n_axes, vmem_limit_bytes=VMEM_LIMIT)


def _sigmoid(x):
    return 1.0 / (1.0 + jnp.exp(-x))


def _silu(x):
    return x * _sigmoid(x)


def _proj_kernel(x_ref, w_ref, *o_refs):
    acc = jnp.dot(x_ref[...].astype(BF16), w_ref[...], preferred_element_type=F32)
    lo = 0
    for o_ref in o_refs:
        width = o_ref.shape[1]
        o_ref[...] = acc[:, lo:lo + width]
        lo += width


def _proj(x, w, tm, widths=None):
    m, k = x.shape
    n = w.shape[1]
    widths = (n,) if widths is None else widths
    assert sum(widths) == n and all(wd % LANES == 0 for wd in widths)
    outs = pl.pallas_call(
        _proj_kernel,
        grid=(m // tm,),
        in_specs=[pl.BlockSpec((tm, k), lambda i: (i, 0)), pl.BlockSpec((k, n), lambda i: (0, 0))],
        out_specs=[pl.BlockSpec((tm, wd), lambda i: (i, 0)) for wd in widths],
        out_shape=[jax.ShapeDtypeStruct((m, wd), F32) for wd in widths],
        compiler_params=_cparams(1),
        name="proj",
    )(x, w)
    return outs[0] if len(widths) == 1 else outs


def _outproj_ln_kernel(y_ref, z_ref, x_ref, w_ref, g_ref, b_ref, o_ref):
    a = (y_ref[...] * _silu(z_ref[...])).astype(BF16)
    h = DEEPNORM_ALPHA * x_ref[...] + jnp.dot(a, w_ref[...], preferred_element_type=F32)
    mu = jnp.mean(h, axis=-1, keepdims=True)
    c = h - mu
    var = jnp.mean(c * c, axis=-1, keepdims=True)
    o_ref[...] = c * lax.rsqrt(var + LN_EPS) * g_ref[...] + b_ref[...]


def _outproj_ln(y, z_arr, z_col, x, w, g, b, tm):
    m = x.shape[0]
    row = lambda i: (i, 0)
    const = lambda i: (0, 0)
    return pl.pallas_call(
        _outproj_ln_kernel,
        grid=(m // tm,),
        in_specs=[
            pl.BlockSpec((tm, D_MODEL), row),
            pl.BlockSpec((tm, D_MODEL), lambda i: (i, z_col)),
            pl.BlockSpec((tm, D_MODEL), row),
            pl.BlockSpec((D_MODEL, D_MODEL), const),
            pl.BlockSpec((1, D_MODEL), const),
            pl.BlockSpec((1, D_MODEL), const),
        ],
        out_specs=pl.BlockSpec((tm, D_MODEL), row),
        out_shape=jax.ShapeDtypeStruct((m, D_MODEL), F32),
        compiler_params=_cparams(1),
        name="outproj_ln",
    )(y, z_arr, x, w, g.reshape(1, D_MODEL), b.reshape(1, D_MODEL))


def _rwkv_prep_kernel(x_ref, win_ref, prev_ref, mu_ref, w0_ref, a0_ref, lora_ref, kk_ref, ka_ref,
                      r_o, w_o, k_o, v_o, kk_o, a_o, z_o, last_o, carry, *, tt, nb, log_decay):
    p = jnp.dot(x_ref[...].astype(BF16), win_ref[...], preferred_element_type=F32)
    seq = tt // nb
    for i in range(nb):
        last_o[i] = p[(i + 1) * seq - 1:(i + 1) * seq, :]
    row = lax.broadcasted_iota(jnp.int32, (tt, 1), 0)
    if nb == 1:
        @pl.when(pl.program_id(1) == 0)
        def _():
            carry[0:1, :] = prev_ref[0]

        before = carry[0:1, :]
        first = row == 0
    else:
        before = jnp.concatenate([jnp.broadcast_to(prev_ref[i], (seq, A_NCOLS)) for i in range(nb)], axis=0)
        first = (row & (seq - 1)) == 0
    p_shift = jnp.where(first, before, pltpu.roll(p, 1, axis=0))
    carry[0:1, :] = p[tt - 1:tt, :]
    pm = p + (p_shift - p) * mu_ref[...]
    r = pm[:, 0:A_WIDTH]
    k = pm[:, A_WIDTH:2 * A_WIDTH]
    v = pm[:, 2 * A_WIDTH:3 * A_WIDTH]
    z = pm[:, 3 * A_WIDTH:4 * A_WIDTH]
    lo = pm[:, 4 * A_WIDTH:A_NCOLS]
    is_w = lax.broadcasted_iota(jnp.int32, (1, 2 * A_LORA), 1) < A_LORA
    lo = jnp.where(is_w, jnp.tanh(lo), lo)
    lora = jnp.dot(lo.astype(BF16), lora_ref[...], preferred_element_type=F32)
    u = w0_ref[...] + lora[:, 0:A_WIDTH]
    w_log = jnp.minimum(u, 0.0) - jnp.log(1.0 + jnp.exp(-jnp.abs(u))) - 0.5
    a = _sigmoid(a0_ref[...] + lora[:, A_WIDTH:2 * A_WIDTH])
    r_o[...] = r
    w_o[...] = -jnp.exp(w_log) if log_decay else jnp.exp(-jnp.exp(w_log))
    k_o[...] = k * (1.0 + (a - 1.0) * ka_ref[...])
    v_o[...] = v
    kk_o[...] = k * kk_ref[...]
    a_o[...] = a
    z_o[...] = z


def _rwkv_prep(x, w_in, p_prev, mu, w0, a0, lora_w, k_k, k_a, batch, seq, tt, log_decay):
    nb = max(1, tt // seq)
    assert (seq % tt == 0 and nb == 1) or (tt % seq == 0 and batch % nb == 0 and seq & (seq - 1) == 0)
    nt = max(1, seq // tt)
    row = lambda b, t: (b * nt + t, 0)
    const = lambda b, t: (0, 0)
    vec = lambda a: a.reshape(1, -1)
    out = jax.ShapeDtypeStruct((batch * seq, A_WIDTH), F32)
    per_seq = pl.BlockSpec((nb, 1, A_NCOLS), lambda b, t: (b, 0, 0))
    return pl.pallas_call(
        functools.partial(_rwkv_prep_kernel, tt=tt, nb=nb, log_decay=log_decay),
        grid=(batch // nb, nt),
        in_specs=[
            pl.BlockSpec((tt, D_MODEL), row),
            pl.BlockSpec((D_MODEL, A_NCOLS), const),
            per_seq,
            pl.BlockSpec((1, A_NCOLS), const),
            pl.BlockSpec((1, A_WIDTH), const),
            pl.BlockSpec((1, A_WIDTH), const),
            pl.BlockSpec((2 * A_LORA, 2 * A_WIDTH), const),
            pl.BlockSpec((1, A_WIDTH), const),
            pl.BlockSpec((1, A_WIDTH), const),
        ],
        out_specs=[pl.BlockSpec((tt, A_WIDTH), row)] * 7 + [per_seq],
        out_shape=[out] * 7 + [jax.ShapeDtypeStruct((batch, 1, A_NCOLS), F32)],
        scratch_shapes=[pltpu.VMEM((8, A_NCOLS), F32)],
        compiler_params=_cparams(2),
        name="rwkv_prep",
    )(x, w_in, p_prev.reshape(batch, 1, A_NCOLS), vec(mu), vec(w0), vec(a0), lora_w, vec(k_k), vec(k_a))


def _split3(x):
    hi = x.astype(BF16)
    rest = x - hi.astype(F32)
    mid = rest.astype(BF16)
    return hi, mid, (rest - mid.astype(F32)).astype(BF16)


def _dot_sel(x, sel):
    hi, mid, lo = _split3(x)
    n = x.shape[0]
    if n % 16 == 0:
        out = jnp.dot(jnp.concatenate([hi, mid, lo], axis=0), sel, preferred_element_type=F32)
        return out[0:n] + (out[n:2 * n] + out[2 * n:3 * n])
    d = lambda a: jnp.dot(a, sel, preferred_element_type=F32)
    return d(hi) + (d(mid) + d(lo))


def _rwkv_chunk_kernel(r_ref, lw_ref, k_ref, v_ref, kkr_ref, a_ref, s0_ref, rk_ref, lg_ref, lb_ref,
                       y_ref, sout_ref, state, *, ln, bb):
    c = pl.program_id(1)
    half = A_HEAD_DIM
    n_pairs = A_WIDTH // LANES

    @pl.when(c == 0)
    def _():
        zero = jnp.zeros((half, half), F32)
        for b in range(bb):
            for p in range(n_pairs):
                top = jnp.concatenate([s0_ref[b, 2 * p], zero], axis=1)
                bottom = jnp.concatenate([zero, s0_ref[b, 2 * p + 1]], axis=1)
                state[b, p] = jnp.concatenate([top, bottom], axis=0)

    row = lax.broadcasted_iota(jnp.int32, (ln, 1), 0)
    lo = lax.broadcasted_iota(jnp.int32, (1, LANES), 1) < half
    r2 = lax.broadcasted_iota(jnp.int32, (2 * ln, ln), 0) & (ln - 1)
    c2 = lax.broadcasted_iota(jnp.int32, (2 * ln, ln), 1)
    strict2, incl2 = c2 < r2, c2 <= r2
    rr = lax.broadcasted_iota(jnp.int32, (LANES, LANES), 0)
    cc = lax.broadcasted_iota(jnp.int32, (LANES, LANES), 1)
    same_head = (rr & half) == (cc & half)
    seg = jnp.where(same_head, 1.0, 0.0).astype(BF16)
    rb = lax.broadcasted_iota(jnp.int32, (2 * ln, 2 * ln), 0)
    cb = lax.broadcasted_iota(jnp.int32, (2 * ln, 2 * ln), 1)
    strict_bd = ((cb & (ln - 1)) + jnp.where((rb & ln) == (cb & ln), 0, ln)) < (rb & (ln - 1))
    split = lambda x: jnp.concatenate([jnp.where(lo, x, 0.0), jnp.where(lo, 0.0, x)], axis=0)
    merge = lambda x2: jnp.where(lo, x2[0:ln], x2[ln:2 * ln])
    mm = lambda a, b: jnp.dot(a.astype(BF16), b.astype(BF16), preferred_element_type=F32)

    pairs = range(bb * n_pairs)
    sls = [slice((p % n_pairs) * LANES, (p % n_pairs + 1) * LANES) for p in pairs]
    rws = [slice((p // n_pairs) * ln, (p // n_pairs + 1) * ln) for p in pairs]
    load = lambda ref: [ref[rw, sl] for rw, sl in zip(rws, sls)]
    r, lw, k, v, kkr, a = (load(ref) for ref in (r_ref, lw_ref, k_ref, v_ref, kkr_ref, a_ref))
    norm2 = [_dot_sel(x * x, seg) for x in kkr]
    kk = [x * lax.rsqrt(jnp.maximum(n2, 1e-24)) for x, n2 in zip(kkr, norm2)]
    bv = [x * y for x, y in zip(kk, a)]
    cum = lw
    d = 1
    while d < ln:
        cum = [x + jnp.where(row >= d, pltpu.roll(x, d, axis=0), 0.0) for x in cum]
        d *= 2
    tot = [x[ln - 1:ln, :] for x in cum]
    kkd = [x * jnp.exp(cm - l) for x, cm, l in zip(kk, cum, lw)]
    rd = [x * jnp.exp(cm) for x, cm in zip(r, cum)]
    w_inv = [jnp.exp(-cm) for cm in cum]
    kd = [x * w for x, w in zip(k, w_inv)]
    bd = [x * w for x, w in zip(bv, w_inv)]
    s_old = [state[p // n_pairs, p % n_pairs] for p in pairs]

    xk = [split(x).astype(BF16) for x in kkd]
    xr = [split(x).astype(BF16) for x in rd]
    nil = [jnp.where(strict_bd, -_dot_t(x, split(y).astype(BF16)), 0.0) for x, y in zip(xk, bd)]
    gk = [_dot_t(jnp.concatenate([x, y], axis=0), z.astype(BF16)) for x, y, z in zip(xk, xr, kd)]
    sx = [_dot_t(jnp.concatenate([x, y], axis=0).astype(BF16), s.astype(BF16)) for x, y, s in zip(kkd, rd, s_old)]
    av = [mm(jnp.concatenate([jnp.where(strict2, g[0:2 * ln], 0.0), jnp.where(incl2, g[2 * ln:4 * ln], 0.0)], axis=0), x)
          for g, x in zip(gk, v)]
    u2 = [split(s[0:ln] + merge(x[0:2 * ln])) for s, x in zip(sx, av)]
    power = nil
    step = 1
    while step < ln:
        step *= 2
        if step < ln:
            both = [mm(pw, jnp.concatenate([x, pw], axis=1)) for x, pw in zip(u2, power)]
            u2 = [x + b[:, 0:LANES] for x, b in zip(u2, both)]
            power = [b[:, LANES:] for b in both]
        else:
            u2 = [x + mm(pw, x) for x, pw in zip(u2, power)]
    u = [x[0:ln] + x[ln:2 * ln] for x in u2]
    a_rb = [jnp.where(incl2, _dot_t(x, y.astype(BF16)), 0.0) for x, y in zip(xr, bd)]
    y = [s[ln:2 * ln] + merge(x[2 * ln:4 * ln]) - merge(mm(g, w)) for s, x, g, w in zip(sx, av, a_rb, u)]

    inv_n = 1.0 / half
    for p in pairs:
        sl = sls[p]
        w_rest = jnp.exp(tot[p] - cum[p])
        upd = mm(jnp.concatenate([v[p], u[p]], axis=0).T,
                 jnp.concatenate([k[p] * w_rest, -(bv[p] * w_rest)], axis=0))
        state[p // n_pairs, p % n_pairs] = s_old[p] * jnp.exp(tot[p]) + jnp.where(same_head, upd, 0.0)
        mean = _dot_sel(y[p], seg) * inv_n
        cen = y[p] - mean
        sums = _dot_sel(jnp.concatenate([cen * cen, r[p] * k[p] * rk_ref[:, sl]], axis=0), seg)
        var, bonus = sums[0:ln] * inv_n, sums[ln:2 * ln]
        y_ref[rws[p], sl] = cen * lax.rsqrt(var + A_GN_EPS) * lg_ref[:, sl] + lb_ref[:, sl] + bonus * v[p]

    @pl.when(c == pl.num_programs(1) - 1)
    def _():
        for b in range(bb):
            for p in range(n_pairs):
                sout_ref[b, 2 * p] = state[b, p, 0:half, 0:half]
                sout_ref[b, 2 * p + 1] = state[b, p, half:LANES, half:LANES]


def _rwkv_chunked(r, lw, k, v, kkr, a, s0, r_k, lnx_g, lnx_b, batch, seq, ln):
    nc = seq // ln
    n_pairs = A_WIDTH // LANES
    bb = RWKV_ROWS_PER_STEP if nc == 1 and batch % RWKV_ROWS_PER_STEP == 0 else 1
    row = pl.BlockSpec((bb * ln, A_WIDTH), lambda b, c: (b * nc + c, 0))
    st = pl.BlockSpec((bb, A_HEADS, A_HEAD_DIM, A_HEAD_DIM), lambda b, c: (b, 0, 0, 0))
    par = pl.BlockSpec((1, A_WIDTH), lambda b, c: (0, 0))
    return pl.pallas_call(
        functools.partial(_rwkv_chunk_kernel, ln=ln, bb=bb),
        grid=(batch // bb, nc),
        in_specs=[row] * 6 + [st] + [par] * 3,
        out_specs=[row, st],
        out_shape=[jax.ShapeDtypeStruct((batch * seq, A_WIDTH), F32),
                   jax.ShapeDtypeStruct((batch, A_HEADS, A_HEAD_DIM, A_HEAD_DIM), F32)],
        scratch_shapes=[pltpu.VMEM((bb, n_pairs, LANES, LANES), F32)],
        compiler_params=_cparams(2),
        name="rwkv_chunk",
    )(r, lw, k, v, kkr, a, s0, r_k.reshape(1, A_WIDTH), lnx_g.reshape(1, A_WIDTH), lnx_b.reshape(1, A_WIDTH))


def _rwkv_layer(x, batch, seq, s0, p_prev, prm, ln_g, ln_b, tm, tt_prep):
    (w_in, mu, w0, w2, a0, a2, k_k, k_a, r_k, lnx_g, lnx_b, w_out) = prm
    ln = min(RWKV_CHUNK, seq)
    assert seq % ln == 0 and ln % 8 == 0 and ln & (ln - 1) == 0
    zeros = jnp.zeros((A_LORA, A_WIDTH), F32)
    lora_w = jnp.concatenate([jnp.concatenate([w2, zeros], axis=1), jnp.concatenate([zeros, a2], axis=1)], axis=0)
    r, w, k, v, kkr, a, z, p_last = _rwkv_prep(x, w_in.astype(BF16), p_prev, mu, w0, a0, lora_w.astype(BF16), k_k,
                                               k_a, batch, seq, tt_prep, True)
    y, s_final = _rwkv_chunked(r, w, k, v, kkr, a, s0, r_k, lnx_g, lnx_b, batch, seq, ln)
    x_new = _outproj_ln(y, z, 0, x, w_out.astype(BF16), ln_g, ln_b, tm)
    return x_new, s_final, p_last.reshape(batch, A_NCOLS)


def _pool_kernel(u_ref, buf_ref, wg_ref, scale_ref, y_ref, nbuf_ref, ext, *, tt, front_valid):
    t = pl.program_id(1)

    @pl.when(t == 0)
    def _():
        ext[1:B_CARRY, :] = buf_ref[0]

    @pl.when(t > 0)
    def _():
        ext[0:B_CARRY, :] = ext[tt:tt + B_CARRY, :]

    u = u_ref[...]
    ext[B_CARRY:B_CARRY + tt, :] = u
    pos = t * tt + lax.broadcasted_iota(jnp.int32, (tt, 1), 0)
    for gi, win in enumerate(B_WINDOWS):
        lo, hi = gi * B_GROUP_W, (gi + 1) * B_GROUP_W
        ug = u[:, lo:hi]
        acc = ug
        for s in range(1, win):
            acc = acc + ext[B_CARRY - s:B_CARRY - s + tt, lo:hi]
        if front_valid:
            cnt = float(win)
        else:
            cnt = jnp.minimum(pos + 1, win).astype(F32)
        d = acc / cnt - ug
        yg = jnp.dot(d.astype(BF16), wg_ref[gi], preferred_element_type=F32)
        y_ref[:, lo:hi] = yg * scale_ref[:, lo:hi]

    @pl.when(t == pl.num_programs(1) - 1)
    def _():
        nbuf_ref[0] = ext[tt + 1:tt + B_CARRY, :]


def _pool(uz, buf, w_grp, scale, batch, seq, tt, front_valid):
    nt = seq // tt
    return pl.pallas_call(
        functools.partial(_pool_kernel, tt=tt, front_valid=front_valid),
        grid=(batch, nt),
        in_specs=[
            pl.BlockSpec((tt, B_WIDTH), lambda b, t: (b * nt + t, 0)),
            pl.BlockSpec((1, B_BUF, B_WIDTH), lambda b, t: (b, 0, 0)),
            pl.BlockSpec((len(B_WINDOWS), B_GROUP_W, B_GROUP_W), lambda b, t: (0, 0, 0)),
            pl.BlockSpec((1, B_WIDTH), lambda b, t: (0, 0)),
        ],
        out_specs=[
            pl.BlockSpec((tt, B_WIDTH), lambda b, t: (b * nt + t, 0)),
            pl.BlockSpec((1, B_BUF, B_WIDTH), lambda b, t: (b, 0, 0)),
        ],
        out_shape=[jax.ShapeDtypeStruct((batch * seq, B_WIDTH), F32),
                   jax.ShapeDtypeStruct((batch, B_BUF, B_WIDTH), F32)],
        scratch_shapes=[pltpu.VMEM((tt + B_CARRY, B_WIDTH), F32)],
        compiler_params=_cparams(2),
        name="pool",
    )(uz, buf, w_grp, scale.reshape(1, B_WIDTH))


def _pool_layer(x, batch, seq, buf, front_valid, prm, ln_g, ln_b, tm, tt):
    w_in, w_grp, scale, w_out = prm
    uz = _proj(x, w_in.astype(BF16), tm)
    y, new_buf = _pool(uz, buf, w_grp.astype(BF16), scale, batch, seq, tt, front_valid)
    x_new = _outproj_ln(y, uz, 1, x, w_out.astype(BF16), ln_g, ln_b, tm)
    return x_new, new_buf


NEG = -1e30
C_SCALE = C_HEAD_DIM ** -0.5


def _alibi_slopes():
    return jnp.power(2.0, -8.0 * (jnp.arange(C_HEADS, dtype=F32) + 1.0) / C_HEADS)


def _dot_t(a, b):
    return lax.dot_general(a, b, (((1,), (1,)), ((), ())), preferred_element_type=F32)


def _masked_softmax_parts(scores, masks, axis=1):
    top = None
    for s, k in zip(scores, masks):
        part = jnp.where(k, s, -jnp.inf)
        top = part if top is None else jnp.maximum(top, part)
    m = jnp.max(top, axis=axis, keepdims=True)
    m = jnp.where(jnp.isfinite(m), m, 0.0)
    es = [jnp.where(k, jnp.exp(s - m), 0.0) for s, k in zip(scores, masks)]
    total = es[0]
    for e in es[1:]:
        total = total + e
    den = jnp.maximum(jnp.sum(total, axis=axis, keepdims=True), 1e-30)
    return [e / den for e in es]


def _top_n_mask(imp, top_n):
    n = imp.shape[1]
    idx = lax.broadcasted_iota(jnp.int32, (1, n), 1)
    rank = jnp.zeros(imp.shape, F32)
    for c in range(n):
        col = imp[:, c:c + 1]
        rank = rank + jnp.where(idx > c, jnp.where(col >= imp, 1.0, 0.0), jnp.where(col > imp, 1.0, 0.0))
    return jnp.where(rank < top_n, jnp.where(imp >= 0.0, 1.0, 0.0), 0.0)


def _top_n_mask_wide(imp, top_n):
    n = imp.shape[1]
    shift = n.bit_length() - 1
    assert n == 1 << shift
    lane = lax.broadcasted_iota(jnp.int32, (n, n * n), 1)
    src = lax.broadcasted_iota(jnp.int32, (n, n * n), 0)
    from_cand = jnp.where(src == lax.shift_right_logical(lane, shift), 1.0, 0.0).astype(BF16)
    from_entry = jnp.where(src == (lane & (n - 1)), 1.0, 0.0).astype(BF16)
    cand = _dot_sel(imp, from_cand)
    entry = _dot_sel(imp, from_entry)
    c_idx = lax.shift_right_logical(lane[0:1], shift)
    s_idx = lane[0:1] & (n - 1)
    beats = jnp.where(s_idx > c_idx, jnp.where(cand >= entry, 1.0, 0.0), jnp.where(cand > entry, 1.0, 0.0))
    rank = _dot_t(beats.astype(BF16), from_entry)
    return jnp.where(rank < top_n, jnp.where(imp >= 0.0, 1.0, 0.0), 0.0)


def _block_expand(n_blocks, kpos):
    blk = lax.broadcasted_iota(jnp.int32, (n_blocks, kpos.shape[1]), 0)
    return jnp.where(lax.shift_right_logical(kpos, 6) == blk, 1.0, 0.0).astype(BF16)


def _cmp_kernel(kv_ref, w_ref, e_ref, o_ref):
    x = kv_ref[...]
    n_pair = x.shape[0] // (2 * C_CMP_BLOCK)
    x4 = x.reshape(n_pair, 2, C_CMP_BLOCK, x.shape[1])
    w = w_ref[...][None]
    e_ref[0] = jnp.sum(x4[:, 0] * w, axis=1)
    o_ref[0] = jnp.sum(x4[:, 1] * w, axis=1)


def _cmp(p, col_block, w, batch, seq):
    width = w.shape[1]
    n_pair = seq // (2 * C_CMP_BLOCK)
    out = jax.ShapeDtypeStruct((batch, n_pair, width), F32)
    return pl.pallas_call(
        _cmp_kernel,
        grid=(batch,),
        in_specs=[pl.BlockSpec((seq, width), lambda b: (b, col_block)), pl.BlockSpec(w.shape, lambda b: (0, 0))],
        out_specs=[pl.BlockSpec((1, n_pair, width), lambda b: (b, 0, 0))] * 2,
        out_shape=[out, out],
        compiler_params=_cparams(1),
        name="nsa_cmp",
    )(p, w)


def _nsa_prompt_kernel(q_ref, g_ref, ce_ref, co_ref, sel_ref, win_ref, slope_ref, o_ref, *, tq, tk):
    qi = pl.program_id(2)
    q0 = qi * tq
    hd = C_HEAD_DIM
    rows = C_HPG * tq
    n_pair = ce_ref.shape[1]
    stack = lambda f: jnp.concatenate([f(j) for j in range(C_HPG)], axis=0)
    tile4 = lambda x: jnp.concatenate([x] * C_HPG, axis=0)
    lo = lax.broadcasted_iota(jnp.int32, (1, LANES), 1) < hd
    q_all = q_ref[...]

    def q_head(j):
        pair_tile = q_all[:, (j // 2) * LANES:(j // 2 + 1) * LANES]
        return jnp.where(lo, pair_tile if j % 2 == 0 else pltpu.roll(pair_tile, hd, axis=1), 0.0)

    qb = (stack(q_head) * C_SCALE).astype(BF16)
    slope = stack(lambda j: jnp.broadcast_to(slope_ref[0, j:j + 1, 0:1], (tq, 1)))
    slope_keys = jnp.broadcast_to(slope, (rows, tk))
    qpos_t = q0 + lax.broadcasted_iota(jnp.int32, (tq, 1), 0)
    qpos = tile4(qpos_t)

    n_gate = 3 * C_HPG
    src = lax.broadcasted_iota(jnp.int32, (LANES, n_gate * LANES), 0)
    dst = lax.shift_right_logical(lax.broadcasted_iota(jnp.int32, (LANES, n_gate * LANES), 1), 7)
    gates = _dot_sel(_sigmoid(g_ref[...]), jnp.where(src == dst, 1.0, 0.0).astype(BF16))
    gate = lambda br: stack(lambda j: gates[:, (3 * j + br) * LANES:(3 * j + br + 1) * LANES])

    pair = lax.broadcasted_iota(jnp.int32, (1, n_pair), 1)
    ce, co = ce_ref[0].astype(BF16), co_ref[0].astype(BF16)
    scores, masks = [], []
    for par, c in ((0, ce), (1, co)):
        dist = qpos - ((2 * pair + par + 1) * C_CMP_BLOCK - 1)
        scores.append(_dot_t(qb, c) - slope * dist.astype(F32))
        masks.append(dist >= 0)
    p_e, p_o = _masked_softmax_parts(scores, masks)
    o_c = (jnp.dot(p_e.astype(BF16), ce, preferred_element_type=F32)
           + jnp.dot(p_o.astype(BF16), co, preferred_element_type=F32))
    imp_h = p_e + p_o
    imp = imp_h[0:tq]
    for j in range(1, C_HPG):
        imp = imp + imp_h[j * tq:(j + 1) * tq]
    cur = lax.shift_right_logical(qpos_t, 6)
    imp = jnp.where(pair == cur, C_FORCE, imp)
    imp = jnp.where(pair <= cur, imp, -1.0)
    sel = _top_n_mask_wide(imp, C_TOP_N).astype(BF16)

    def attend(carry, k0, kv, penalty_of):
        m, acc = carry
        kpos = k0 + lax.broadcasted_iota(jnp.int32, (1, tk), 1)
        dist = qpos_t - kpos
        s = (_dot_t(qb, kv.astype(BF16)) - slope_keys * tile4(dist.astype(F32))) + tile4(penalty_of(dist, kpos))
        m_new = jnp.maximum(m, jnp.max(s, axis=1, keepdims=True))
        p = jnp.exp(s - m_new).astype(BF16)
        ones_v = jnp.where(lo, 1.0, kv).astype(BF16)
        acc = jnp.exp(m - m_new) * acc + jnp.dot(p, ones_v, preferred_element_type=F32)
        return m_new, acc

    def finish(carry):
        _, acc = carry
        return acc / jnp.where(lo, 1.0, jnp.maximum(pltpu.roll(acc, hd, axis=1), 1e-30))

    init = (jnp.full((rows, 1), NEG, F32), jnp.zeros((rows, LANES), F32))

    def sel_step(c, carry):
        k0 = pl.multiple_of(c * tk, tk)

        def penalty(dist, kpos):
            chosen = jnp.dot(sel, _block_expand(n_pair, kpos), preferred_element_type=F32)
            return jnp.where(dist >= 0, jnp.where(chosen > 0.5, 0.0, NEG), NEG)

        return attend(carry, k0, sel_ref[pl.ds(k0, tk), :], penalty)

    o_s = finish(lax.fori_loop(0, qi, sel_step, sel_step(qi, init)))

    w0 = jnp.maximum(q0 - C_WINDOW, 0)
    n_win = (tq + C_WINDOW) // tk
    own = (q0 - w0) // tk

    def win_step(i, carry):
        k0 = pl.multiple_of(w0 + lax.rem(own + i, n_win) * tk, tk)
        penalty = lambda dist, kpos: jnp.where(dist >= 0, jnp.where(dist < C_WINDOW, 0.0, NEG), NEG)
        return attend(carry, k0, win_ref[pl.ds(k0, tk), :], penalty)

    o_w = finish(lax.fori_loop(1, n_win, win_step, win_step(0, init)))

    o = gate(0) * o_c + gate(1) * o_s + gate(2) * o_w
    head = lambda j: o[j * tq:(j + 1) * tq]
    o_ref[...] = jnp.concatenate([jnp.where(lo, pltpu.roll(head(2 * i), hd, axis=1), head(2 * i + 1))
                                  for i in range(C_HPG // 2)], axis=1)


def _nsa_prompt_layer(x, batch, seq, prm, ln_g, ln_b):
    w_in, cmp_wk, cmp_wv, w_out = prm
    hd, grp = C_HEAD_DIM, C_KV_HEADS
    tq = tk = 256
    assert seq % tq == 0 and seq >= tq + C_WINDOW and C_WINDOW % tk == 0 and seq % (2 * C_CMP_BLOCK) == 0
    assert tq == tk and seq == 2 * C_CMP_BLOCK * (seq // C_SEL_BLOCK)
    kv_w = lambda i: w_in[:, C_WIDTH + C_KV_WIDTH * i:C_WIDTH + C_KV_WIDTH * (i + 1)].reshape(D_MODEL, grp, hd)
    kv_pair = lambda a, b: jnp.concatenate([kv_w(a), kv_w(b)], axis=2).reshape(D_MODEL, grp * 2 * hd)
    g_lo = C_WIDTH + 6 * C_KV_WIDTH
    g_w = w_in[:, g_lo:g_lo + 3 * C_HEADS].reshape(D_MODEL, grp, 3 * C_HPG)
    g_w = jnp.pad(g_w, ((0, 0), (0, 0), (0, LANES - 3 * C_HPG))).reshape(D_MODEL, grp * LANES)
    w_p = jnp.concatenate([w_in[:, :C_WIDTH], w_in[:, g_lo + 3 * C_HEADS:], kv_pair(0, 1), kv_pair(2, 3),
                           kv_pair(4, 5), g_w, w_in[:, C_WIDTH:C_WIDTH + 4 * C_KV_WIDTH]], axis=1)
    n_main = 2 * C_WIDTH + 4 * grp * 2 * hd
    p, *new_rows = _proj(x, w_p.astype(BF16), 256, (n_main,) + (C_KV_WIDTH,) * 4)
    kv_lo = 2 * C_WIDTH
    kvw = grp * 2 * hd
    cw = jnp.concatenate([jnp.broadcast_to(cmp_wk[:, None], (C_CMP_BLOCK, hd)),
                          jnp.broadcast_to(cmp_wv[:, None], (C_CMP_BLOCK, hd))], axis=1)
    ce, co = _cmp(p, kv_lo // kvw, jnp.tile(cw, (1, grp)), batch, seq)
    slopes = jnp.broadcast_to(jnp.pad(_alibi_slopes().reshape(grp, C_HPG), ((0, 0), (0, 8 - C_HPG)))[:, :, None],
                              (grp, 8, LANES))
    nq = seq // tq
    n_pair = seq // (2 * C_CMP_BLOCK)
    gw = 2 * hd
    o = pl.pallas_call(
        functools.partial(_nsa_prompt_kernel, tq=tq, tk=tk),
        grid=(batch, grp, nq),
        in_specs=[
            pl.BlockSpec((tq, C_HPG * hd), lambda b, g, i: (b * nq + i, g)),
            pl.BlockSpec((tq, LANES), lambda b, g, i: (b * nq + i, (kv_lo + 3 * kvw) // LANES + g)),
            pl.BlockSpec((1, n_pair, gw), lambda b, g, i: (b, 0, g)),
            pl.BlockSpec((1, n_pair, gw), lambda b, g, i: (b, 0, g)),
            pl.BlockSpec((seq, gw), lambda b, g, i: (b, (kv_lo + kvw) // gw + g)),
            pl.BlockSpec((seq, gw), lambda b, g, i: (b, (kv_lo + 2 * kvw) // gw + g)),
            pl.BlockSpec((1, 8, LANES), lambda b, g, i: (g, 0, 0)),
        ],
        out_specs=pl.BlockSpec((tq, C_HPG * hd), lambda b, g, i: (b * nq + i, g)),
        out_shape=jax.ShapeDtypeStruct((batch * seq, C_WIDTH), F32),
        compiler_params=_cparams(3),
        name="nsa_prompt",
    )(p, p, ce, co, p, p, slopes)
    x_new = _outproj_ln(o, p, 1, x, w_out.astype(BF16), ln_g, ln_b, 512)
    rows = tuple(r.reshape(batch, seq, grp, hd) for r in new_rows)
    keep = min(C_WINDOW, seq)
    win = p.reshape(batch, seq, -1)[:, seq - keep:, kv_lo + 2 * kvw:kv_lo + 3 * kvw].reshape(batch, keep, grp, 2, hd)
    return x_new, rows, win[:, :, :, 0], win[:, :, :, 1]


def _nsa_sample_kernel(tbl_ref, *refs, n_pages, ts):
    del tbl_ref
    ck, cv, sk, sv = (refs[i * n_pages:(i + 1) * n_pages] for i in range(4))
    (q_ref, g_ref, ksn_ref, vsn_ref, kwn_ref, vwn_ref, wk_ref, wv_ref, cwk_ref, cwv_ref, slope_ref,
     o_ref, ke_s, ko_s, ve_s, vo_s, new_s) = refs[4 * n_pages:]
    hd, grp = C_HEAD_DIM, C_KV_HEADS
    rows = C_HEADS * ts
    past = n_pages * PAGE_SIZE
    n_pair = past // C_SEL_BLOCK
    qpos = past + lax.rem(lax.broadcasted_iota(jnp.int32, (rows, 1), 0), ts)
    qb = (q_ref[0] * C_SCALE).astype(BF16)
    slope = slope_ref[:, 0:1]

    per_page = PAGE_SIZE // C_CMP_BLOCK
    for pages, cw_ref, e_s, o_s in ((ck, cwk_ref, ke_s, ko_s), (cv, cwv_ref, ve_s, vo_s)):
        cw = cw_ref[...][None]
        for pg in range(n_pages):
            tok = jnp.sum(pages[pg][0].reshape(per_page, C_CMP_BLOCK, grp * hd) * cw, axis=1)
            for i in range(per_page):
                n = pg * per_page + i
                dst = e_s if n % 2 == 0 else o_s
                dst[n // 2:n // 2 + 1, :] = tok[i:i + 1]

    pair = lax.broadcasted_iota(jnp.int32, (1, n_pair), 1)
    scores, masks = [], []
    for par, k_s in ((0, ke_s), (1, ko_s)):
        dist = qpos - ((2 * pair + par + 1) * C_CMP_BLOCK - 1)
        scores.append(_dot_t(qb, k_s[...].astype(BF16)) - slope * dist.astype(F32))
        masks.append(dist >= 0)
    p_e, p_o = _masked_softmax_parts(scores, masks)
    o_c = (jnp.dot(p_e.astype(BF16), ve_s[...].astype(BF16), preferred_element_type=F32)
           + jnp.dot(p_o.astype(BF16), vo_s[...].astype(BF16), preferred_element_type=F32))
    imp_h = p_e + p_o
    sel_rows = []
    for g in range(grp):
        base = g * C_HPG * ts
        imp = imp_h[base:base + ts]
        for j in range(1, C_HPG):
            imp = imp + imp_h[base + j * ts:base + (j + 1) * ts]
        sel_g = _top_n_mask(imp, C_TOP_N - 1)
        sel_rows += [sel_g] * C_HPG
    sel_rows = jnp.concatenate(sel_rows, axis=0).astype(BF16)

    lane_pos = lax.broadcasted_iota(jnp.int32, (1, PAGE_SIZE), 1)

    def new_rows(slot, k_ref, v_ref):
        new_s[2 * slot:2 * slot + 2] = jnp.zeros((2,) + new_s.shape[1:], F32)
        new_s[2 * slot, 0:ts, :] = k_ref[...]
        new_s[2 * slot + 1, 0:ts, :] = v_ref[...]
        return new_s[2 * slot].astype(BF16), new_s[2 * slot + 1].astype(BF16)

    def softmax_attend(tiles):
        scores = []
        for kb, _, kpos, pen in tiles:
            dist = qpos - kpos
            scores.append((_dot_t(qb, kb) - slope * dist.astype(F32)) + pen(dist))
        top = scores[0]
        for s in scores[1:]:
            top = jnp.maximum(top, s)
        m = jnp.max(top, axis=1, keepdims=True)
        probs = [jnp.exp(s - m) for s in scores]
        total = probs[0]
        for p in probs[1:]:
            total = total + p
        acc = None
        for p, (_, vb, _, _) in zip(probs, tiles):
            part = jnp.dot(p.astype(BF16), vb, preferred_element_type=F32)
            acc = part if acc is None else acc + part
        return acc / jnp.maximum(jnp.sum(total, axis=1, keepdims=True), 1e-30)

    causal = lambda dist: jnp.where(dist >= 0, 0.0, NEG)
    window = lambda dist: jnp.where(dist >= 0, jnp.where(dist < C_WINDOW, 0.0, NEG), NEG)

    tiles = []
    for pg in range(n_pages):
        kpos = pg * PAGE_SIZE + lane_pos
        chosen = jnp.dot(sel_rows, _block_expand(n_pair, kpos), preferred_element_type=F32)
        pen = lambda dist, chosen=chosen: jnp.where(dist >= 0, jnp.where(chosen > 0.5, 0.0, NEG), NEG)
        tiles.append((sk[pg][0].astype(BF16), sv[pg][0].astype(BF16), kpos, pen))
    tiles.append(new_rows(0, ksn_ref, vsn_ref) + (past + lane_pos, causal))
    o_s = softmax_attend(tiles)

    n_win = wk_ref.shape[1]
    tiles = []
    for c in range(n_win // PAGE_SIZE):
        lo = c * PAGE_SIZE
        tiles.append((wk_ref[0, lo:lo + PAGE_SIZE, :].astype(BF16), wv_ref[0, lo:lo + PAGE_SIZE, :].astype(BF16),
                      past - n_win + lo + lane_pos, window))
    tiles.append(new_rows(1, kwn_ref, vwn_ref) + (past + lane_pos, window))
    o_w = softmax_attend(tiles)

    per_group = C_HPG * ts
    diag = lambda full: jnp.concatenate(
        [full[g * per_group:(g + 1) * per_group, g * hd:(g + 1) * hd] for g in range(grp)], axis=0)
    gates = _sigmoid(g_ref[0])
    o_ref[0] = gates[:, 0:1] * diag(o_c) + gates[:, 1:2] * diag(o_s) + gates[:, 2:3] * diag(o_w)


def _nsa_sample_layer(x, batch, ts, caches, page_table, win_k, win_v, prm, ln_g, ln_b):
    w_in, cmp_wk, cmp_wv, w_out = prm
    hd, grp = C_HEAD_DIM, C_KV_HEADS
    n_pages = page_table.shape[1]
    n_win = win_k.shape[1]
    past = n_pages * PAGE_SIZE
    rows = C_HEADS * ts
    assert past % C_SEL_BLOCK == 0 and ts <= C_SEL_BLOCK and ts % 8 == 0 and n_win == C_WINDOW and past >= n_win
    assert ts & (ts - 1) == 0 and ts <= 16 and rows == LANES
    g_lo = C_WIDTH + 6 * C_KV_WIDTH
    w_s = jnp.concatenate([w_in[:, :C_WIDTH], w_in[:, g_lo + 3 * C_HEADS:], w_in[:, C_WIDTH:g_lo],
                           jnp.pad(w_in[:, g_lo:g_lo + 3 * C_HEADS], ((0, 0), (0, LANES - 3 * C_HEADS)))], axis=1)
    p = _proj(x, w_s.astype(BF16), 512)
    kv_lo = 2 * C_WIDTH
    q = p[:, :C_WIDTH].reshape(batch, ts, grp, C_HPG, hd).transpose(0, 2, 3, 1, 4).reshape(batch, grp, C_HPG * ts, hd)
    q_bd = (q[:, :, :, None, :] * jnp.eye(grp, dtype=F32)[None, :, None, :, None]).reshape(batch, rows, grp * hd)
    g_lo_p = kv_lo + 6 * C_KV_WIDTH
    g_t = p[:, g_lo_p:g_lo_p + 3 * C_HEADS].reshape(batch, ts, C_HEADS, 3).transpose(0, 2, 1, 3).reshape(batch, rows, 3)
    slope_rows = jnp.broadcast_to(jnp.repeat(_alibi_slopes(), ts)[:, None], (rows, LANES))
    bcast = lambda w: jnp.broadcast_to(w[:, None], (C_CMP_BLOCK, grp * hd))
    pools = [c.reshape(c.shape[0], PAGE_SIZE, grp * hd) for c in caches]
    page_spec = lambda pg: pl.BlockSpec((1, PAGE_SIZE, grp * hd), lambda b, tbl: (tbl[b, pg], 0, 0))
    new_spec = lambda i: pl.BlockSpec((ts, C_KV_WIDTH), lambda b, tbl: (b, kv_lo // C_KV_WIDTH + i))
    const2 = lambda shape: pl.BlockSpec(shape, lambda b, tbl: (0, 0))
    per_b = lambda shape: pl.BlockSpec((1,) + shape, lambda b, tbl: (b, 0, 0))
    in_specs = ([page_spec(pg) for _ in range(4) for pg in range(n_pages)]
                + [per_b((rows, grp * hd)), per_b((rows, 3)), new_spec(2), new_spec(3), new_spec(4), new_spec(5),
                   per_b((n_win, grp * hd)), per_b((n_win, grp * hd)),
                   const2((C_CMP_BLOCK, grp * hd)), const2((C_CMP_BLOCK, grp * hd)), const2((rows, LANES))])
    n_pair = past // C_SEL_BLOCK
    o = pl.pallas_call(
        functools.partial(_nsa_sample_kernel, n_pages=n_pages, ts=ts),
        grid_spec=pltpu.PrefetchScalarGridSpec(
            num_scalar_prefetch=1,
            grid=(batch,),
            in_specs=in_specs,
            out_specs=per_b((rows, hd)),
            scratch_shapes=[pltpu.VMEM((n_pair, grp * hd), F32)] * 4 + [pltpu.VMEM((4, PAGE_SIZE, grp * hd), F32)],
        ),
        out_shape=jax.ShapeDtypeStruct((batch, rows, hd), F32),
        compiler_params=_cparams(1),
        name="nsa_sample",
    )(page_table, *[pool for pool in pools for _ in range(n_pages)], q_bd, g_t, p, p, p, p,
      win_k.reshape(batch, n_win, grp * hd), win_v.reshape(batch, n_win, grp * hd), bcast(cmp_wk), bcast(cmp_wv),
      slope_rows)
    new = lambda i: p[:, kv_lo + i * C_KV_WIDTH:kv_lo + (i + 1) * C_KV_WIDTH].reshape(batch, ts, grp, hd)
    y = o.reshape(batch, C_HEADS, ts, hd).transpose(0, 2, 1, 3).reshape(batch * ts, C_WIDTH)
    x_new = _outproj_ln(y, p, 1, x, w_out.astype(BF16), ln_g, ln_b, 512)
    keep = lambda buf, i: jnp.concatenate([buf, new(i)], axis=1)[:, -n_win:]
    return x_new, (new(0), new(1), new(2), new(3)), keep(win_k, 4), keep(win_v, 5)


def kernel(x_prompt, x_sample, state_rwkv_S, state_rwkv_shift, state_pool, cache_cmp_k, cache_cmp_v, cache_sel_k,
           cache_sel_v, state_win_k, state_win_v, page_table, ln_g, ln_b, a_w_in, a_mu, a_w0, a_w2, a_a0, a_a2, a_k_k,
           a_k_a, a_r_k, a_lnx_g, a_lnx_b, a_w_out, b_w_in, b_w_grp, b_scale, b_w_out, c_w_in, c_cmp_wk, c_cmp_wv,
           c_w_out):
    bp, tp, _ = x_prompt.shape
    bs, ts, _ = x_sample.shape
    xp = x_prompt.reshape(bp * tp, D_MODEL)
    xs = x_sample.reshape(bs * ts, D_MODEL)
    s_p, s_s, sh_p, sh_s, pl_p, pl_s = [], [], [], [], [], []
    rows_p, rows_s, wk_p, wk_s, wv_p, wv_s = [], [], [], [], [], []
    for layer in range(DEPTH):
        kind, li = layer % 3, layer // 3
        g, b = ln_g[layer], ln_b[layer]
        if kind == 0:
            prm = (a_w_in[li], a_mu[li], a_w0[li], a_w2[li], a_a0[li], a_a2[li], a_k_k[li], a_k_a[li], a_r_k[li],
                   a_lnx_g[li], a_lnx_b[li], a_w_out[li])
            xp, s_new, sh_new = _rwkv_layer(xp, bp, tp, jnp.zeros((bp, A_HEADS, A_HEAD_DIM, A_HEAD_DIM), F32),
                                            jnp.zeros((bp, A_NCOLS), F32), prm, g, b, 512, 256)
            s_p.append(s_new)
            sh_p.append(sh_new)
            xs, s_new, sh_new = _rwkv_layer(xs, bs, ts, state_rwkv_S[li], state_rwkv_shift[li], prm, g, b, 512, 64)
            s_s.append(s_new)
            sh_s.append(sh_new)
        elif kind == 1:
            prm = (b_w_in[li], b_w_grp[li], b_scale[li], b_w_out[li])
            xp, buf_new = _pool_layer(xp, bp, tp, jnp.zeros((bp, B_BUF, B_WIDTH), F32), False, prm, g, b, 512, 512)
            pl_p.append(buf_new)
            xs, buf_new = _pool_layer(xs, bs, ts, state_pool[li], True, prm, g, b, 512, ts)
            pl_s.append(buf_new)
        else:
            prm = (c_w_in[li], c_cmp_wk[li], c_cmp_wv[li], c_w_out[li])
            xp, rows, wk, wv = _nsa_prompt_layer(xp, bp, tp, prm, g, b)
            rows_p.append(rows)
            wk_p.append(wk)
            wv_p.append(wv)
            caches = (cache_cmp_k[li], cache_cmp_v[li], cache_sel_k[li], cache_sel_v[li])
            xs, rows, wk, wv = _nsa_sample_layer(xs, bs, ts, caches, page_table, state_win_k[li], state_win_v[li],
                                                 prm, g, b)
            rows_s.append(rows)
            wk_s.append(wk)
            wv_s.append(wv)
    stack = jnp.stack
    return (xp.reshape(bp, tp, D_MODEL), xs.reshape(bs, ts, D_MODEL), stack(s_p), stack(s_s), stack(sh_p), stack(sh_s),
            stack(pl_p), stack(pl_s),
            stack([r[0] for r in rows_p]), stack([r[0] for r in rows_s]),
            stack([r[1] for r in rows_p]), stack([r[1] for r in rows_s]),
            stack([r[2] for r in rows_p]), stack([r[2] for r in rows_s]),
            stack([r[3] for r in rows_p]), stack([r[3] for r in rows_s]),
            stack(wk_p), stack(wk_s), stack(wv_p), stack(wv_s))
```

```python
import functools

import jax
import jax.numpy as jnp
import numpy as np
from jax import lax
from jax.experimental import pallas as pl
from jax.experimental.pallas import tpu as pltpu

F32 = jnp.float32
BF16 = jnp.bfloat16

D_MODEL = 1024
DEPTH = 4
DEEPNORM_ALPHA = (2.0 * DEPTH) ** 0.25
LN_EPS = 1e-5

A_HEADS = 16
A_HEAD_DIM = 64
A_WIDTH = 1024
A_LORA = 64
A_NCOLS = 4 * A_WIDTH + 2 * A_LORA
A_GN_EPS = 64e-5
RWKV_CHUNK = 64
RWKV_ROWS_PER_STEP = 4

B_WIDTH = 1024
B_GROUP_W = 256
B_WINDOWS = (2, 4, 8, 16)
B_BUF = 15
B_CARRY = 16

C_HEADS = 16
C_KV_HEADS = 4
C_HPG = 4
C_HEAD_DIM = 64
C_WIDTH = 1024
C_KV_WIDTH = 256
C_CMP_BLOCK = 32
C_SEL_BLOCK = 64
C_TOP_N = 16
C_WINDOW = 512
C_FORCE = 1e9
PAGE_SIZE = 128

LANES = 128
VMEM_LIMIT = 56 * 1024 * 1024


def _cparams(n_axes):
    return pltpu.CompilerParams(dimension_semantics=("arbitrary",) * n_axes, vmem_limit_bytes=VMEM_LIMIT)


def _sigmoid(x):
    return 1.0 / (1.0 + jnp.exp(-x))


def _silu(x):
    return x * _sigmoid(x)


def _proj_kernel(x_ref, w_ref, *o_refs):
    acc = jnp.dot(x_ref[...].astype(BF16), w_ref[...], preferred_element_type=F32)
    lo = 0
    for o_ref in o_refs:
        width = o_ref.shape[1]
        o_ref[...] = acc[:, lo:lo + width]
        lo += width


def _proj(x, w, tm, widths=None):
    m, k = x.shape
    n = w.shape[1]
    widths = (n,) if widths is None else widths
    assert sum(widths) == n and all(wd % LANES == 0 for wd in widths)
    outs = pl.pallas_call(
        _proj_kernel,
        grid=(m // tm,),
        in_specs=[pl.BlockSpec((tm, k), lambda i: (i, 0)), pl.BlockSpec((k, n), lambda i: (0, 0))],
        out_specs=[pl.BlockSpec((tm, wd), lambda i: (i, 0)) for wd in widths],
        out_shape=[jax.ShapeDtypeStruct((m, wd), F32) for wd in widths],
        compiler_params=_cparams(1),
        name="proj",
    )(x, w)
    return outs[0] if len(widths) == 1 else outs


def _outproj_ln_kernel(y_ref, z_ref, x_ref, w_ref, g_ref, b_ref, o_ref):
    a = (y_ref[...] * _silu(z_ref[...])).astype(BF16)
    h = DEEPNORM_ALPHA * x_ref[...] + jnp.dot(a, w_ref[...], preferred_element_type=F32)
    mu = jnp.mean(h, axis=-1, keepdims=True)
    c = h - mu
    var = jnp.mean(c * c, axis=-1, keepdims=True)
    o_ref[...] = c * lax.rsqrt(var + LN_EPS) * g_ref[...] + b_ref[...]


def _outproj_ln(y, z_arr, z_col, x, w, g, b, tm):
    m = x.shape[0]
    row = lambda i: (i, 0)
    const = lambda i: (0, 0)
    return pl.pallas_call(
        _outproj_ln_kernel,
        grid=(m // tm,),
        in_specs=[
            pl.BlockSpec((tm, D_MODEL), row),
            pl.BlockSpec((tm, D_MODEL), lambda i: (i, z_col)),
            pl.BlockSpec((tm, D_MODEL), row),
            pl.BlockSpec((D_MODEL, D_MODEL), const),
            pl.BlockSpec((1, D_MODEL), const),
            pl.BlockSpec((1, D_MODEL), const),
        ],
        out_specs=pl.BlockSpec((tm, D_MODEL), row),
        out_shape=jax.ShapeDtypeStruct((m, D_MODEL), F32),
        compiler_params=_cparams(1),
        name="outproj_ln",
    )(y, z_arr, x, w, g.reshape(1, D_MODEL), b.reshape(1, D_MODEL))


def _rwkv_prep_kernel(x_ref, win_ref, prev_ref, mu_ref, w0_ref, a0_ref, lora_ref, kk_ref, ka_ref,
                      r_o, w_o, k_o, v_o, kk_o, a_o, z_o, last_o, carry, *, tt, nb, log_decay):
    p = jnp.dot(x_ref[...].astype(BF16), win_ref[...], preferred_element_type=F32)
    seq = tt // nb
    for i in range(nb):
        last_o[i] = p[(i + 1) * seq - 1:(i + 1) * seq, :]
    row = lax.broadcasted_iota(jnp.int32, (tt, 1), 0)
    if nb == 1:
        @pl.when(pl.program_id(1) == 0)
        def _():
            carry[0:1, :] = prev_ref[0]

        before = carry[0:1, :]
        first = row == 0
    else:
        before = jnp.concatenate([jnp.broadcast_to(prev_ref[i], (seq, A_NCOLS)) for i in range(nb)], axis=0)
        first = (row & (seq - 1)) == 0
    p_shift = jnp.where(first, before, pltpu.roll(p, 1, axis=0))
    carry[0:1, :] = p[tt - 1:tt, :]
    pm = p + (p_shift - p) * mu_ref[...]
    r = pm[:, 0:A_WIDTH]
    k = pm[:, A_WIDTH:2 * A_WIDTH]
    v = pm[:, 2 * A_WIDTH:3 * A_WIDTH]
    z = pm[:, 3 * A_WIDTH:4 * A_WIDTH]
    lo = pm[:, 4 * A_WIDTH:A_NCOLS]
    is_w = lax.broadcasted_iota(jnp.int32, (1, 2 * A_LORA), 1) < A_LORA
    lo = jnp.where(is_w, jnp.tanh(lo), lo)
    lora = jnp.dot(lo.astype(BF16), lora_ref[...], preferred_element_type=F32)
    u = w0_ref[...] + lora[:, 0:A_WIDTH]
    w_log = jnp.minimum(u, 0.0) - jnp.log(1.0 + jnp.exp(-jnp.abs(u))) - 0.5
    a = _sigmoid(a0_ref[...] + lora[:, A_WIDTH:2 * A_WIDTH])
    r_o[...] = r
    w_o[...] = -jnp.exp(w_log) if log_decay else jnp.exp(-jnp.exp(w_log))
    k_o[...] = k * (1.0 + (a - 1.0) * ka_ref[...])
    v_o[...] = v
    kk_o[...] = k * kk_ref[...]
    a_o[...] = a
    z_o[...] = z


def _rwkv_prep(x, w_in, p_prev, mu, w0, a0, lora_w, k_k, k_a, batch, seq, tt, log_decay):
    nb = max(1, tt // seq)
    assert (seq % tt == 0 and nb == 1) or (tt % seq == 0 and batch % nb == 0 and seq & (seq - 1) == 0)
    nt = max(1, seq // tt)
    row = lambda b, t: (b * nt + t, 0)
    const = lambda b, t: (0, 0)
    vec = lambda a: a.reshape(1, -1)
    out = jax.ShapeDtypeStruct((batch * seq, A_WIDTH), F32)
    per_seq = pl.BlockSpec((nb, 1, A_NCOLS), lambda b, t: (b, 0, 0))
    return pl.pallas_call(
        functools.partial(_rwkv_prep_kernel, tt=tt, nb=nb, log_decay=log_decay),
        grid=(batch // nb, nt),
        in_specs=[
            pl.BlockSpec((tt, D_MODEL), row),
            pl.BlockSpec((D_MODEL, A_NCOLS), const),
            per_seq,
            pl.BlockSpec((1, A_NCOLS), const),
            pl.BlockSpec((1, A_WIDTH), const),
            pl.BlockSpec((1, A_WIDTH), const),
            pl.BlockSpec((2 * A_LORA, 2 * A_WIDTH), const),
            pl.BlockSpec((1, A_WIDTH), const),
            pl.BlockSpec((1, A_WIDTH), const),
        ],
        out_specs=[pl.BlockSpec((tt, A_WIDTH), row)] * 7 + [per_seq],
        out_shape=[out] * 7 + [jax.ShapeDtypeStruct((batch, 1, A_NCOLS), F32)],
        scratch_shapes=[pltpu.VMEM((8, A_NCOLS), F32)],
        compiler_params=_cparams(2),
        name="rwkv_prep",
    )(x, w_in, p_prev.reshape(batch, 1, A_NCOLS), vec(mu), vec(w0), vec(a0), lora_w, vec(k_k), vec(k_a))


def _split3(x):
    hi = x.astype(BF16)
    rest = x - hi.astype(F32)
    mid = rest.astype(BF16)
    return hi, mid, (rest - mid.astype(F32)).astype(BF16)


def _dot_sel(x, sel):
    hi, mid, lo = _split3(x)
    n = x.shape[0]
    if n % 16 == 0:
        out = jnp.dot(jnp.concatenate([hi, mid, lo], axis=0), sel, preferred_element_type=F32)
        return out[0:n] + (out[n:2 * n] + out[2 * n:3 * n])
    d = lambda a: jnp.dot(a, sel, preferred_element_type=F32)
    return d(hi) + (d(mid) + d(lo))


def _rwkv_chunk_kernel(r_ref, lw_ref, k_ref, v_ref, kkr_ref, a_ref, s0_ref, rk_ref, lg_ref, lb_ref,
                       y_ref, sout_ref, state, *, ln, bb):
    c = pl.program_id(1)
    half = A_HEAD_DIM
    n_pairs = A_WIDTH // LANES

    @pl.when(c == 0)
    def _():
        zero = jnp.zeros((half, half), F32)
        for b in range(bb):
            for p in range(n_pairs):
                top = jnp.concatenate([s0_ref[b, 2 * p], zero], axis=1)
                bottom = jnp.concatenate([zero, s0_ref[b, 2 * p + 1]], axis=1)
                state[b, p] = jnp.concatenate([top, bottom], axis=0)

    row = lax.broadcasted_iota(jnp.int32, (ln, 1), 0)
    lo = lax.broadcasted_iota(jnp.int32, (1, LANES), 1) < half
    r2 = lax.broadcasted_iota(jnp.int32, (2 * ln, ln), 0) & (ln - 1)
    c2 = lax.broadcasted_iota(jnp.int32, (2 * ln, ln), 1)
    strict2, incl2 = c2 < r2, c2 <= r2
    rr = lax.broadcasted_iota(jnp.int32, (LANES, LANES), 0)
    cc = lax.broadcasted_iota(jnp.int32, (LANES, LANES), 1)
    same_head = (rr & half) == (cc & half)
    seg = jnp.where(same_head, 1.0, 0.0).astype(BF16)
    rb = lax.broadcasted_iota(jnp.int32, (2 * ln, 2 * ln), 0)
    cb = lax.broadcasted_iota(jnp.int32, (2 * ln, 2 * ln), 1)
    strict_bd = ((cb & (ln - 1)) + jnp.where((rb & ln) == (cb & ln), 0, ln)) < (rb & (ln - 1))
    split = lambda x: jnp.concatenate([jnp.where(lo, x, 0.0), jnp.where(lo, 0.0, x)], axis=0)
    merge = lambda x2: jnp.where(lo, x2[0:ln], x2[ln:2 * ln])
    mm = lambda a, b: jnp.dot(a.astype(BF16), b.astype(BF16), preferred_element_type=F32)

    pairs = range(bb * n_pairs)
    sls = [slice((p % n_pairs) * LANES, (p % n_pairs + 1) * LANES) for p in pairs]
    rws = [slice((p // n_pairs) * ln, (p // n_pairs + 1) * ln) for p in pairs]
    load = lambda ref: [ref[rw, sl] for rw, sl in zip(rws, sls)]
    r, lw, k, v, kkr, a = (load(ref) for ref in (r_ref, lw_ref, k_ref, v_ref, kkr_ref, a_ref))
    norm2 = [_dot_sel(x * x, seg) for x in kkr]
    kk = [x * lax.rsqrt(jnp.maximum(n2, 1e-24)) for x, n2 in zip(kkr, norm2)]
    bv = [x * y for x, y in zip(kk, a)]
    cum = lw
    d = 1
    while d < ln:
        cum = [x + jnp.where(row >= d, pltpu.roll(x, d, axis=0), 0.0) for x in cum]
        d *= 2
    tot = [x[ln - 1:ln, :] for x in cum]
    kkd = [x * jnp.exp(cm - l) for x, cm, l in zip(kk, cum, lw)]
    rd = [x * jnp.exp(cm) for x, cm in zip(r, cum)]
    w_inv = [jnp.exp(-cm) for cm in cum]
    kd = [x * w for x, w in zip(k, w_inv)]
    bd = [x * w for x, w in zip(bv, w_inv)]
    s_old = [state[p // n_pairs, p % n_pairs] for p in pairs]

    xk = [split(x).astype(BF16) for x in kkd]
    xr = [split(x).astype(BF16) for x in rd]
    nil = [jnp.where(strict_bd, -_dot_t(x, split(y).astype(BF16)), 0.0) for x, y in zip(xk, bd)]
    gk = [_dot_t(jnp.concatenate([x, y], axis=0), z.astype(BF16)) for x, y, z in zip(xk, xr, kd)]
    sx = [_dot_t(jnp.concatenate([x, y], axis=0).astype(BF16), s.astype(BF16)) for x, y, s in zip(kkd, rd, s_old)]
    av = [mm(jnp.concatenate([jnp.where(strict2, g[0:2 * ln], 0.0), jnp.where(incl2, g[2 * ln:4 * ln], 0.0)], axis=0), x)
          for g, x in zip(gk, v)]
    u2 = [split(s[0:ln] + merge(x[0:2 * ln])) for s, x in zip(sx, av)]
    power = nil
    step = 1
    while step < ln:
        step *= 2
        if step < ln:
            both = [mm(pw, jnp.concatenate([x, pw], axis=1)) for x, pw in zip(u2, power)]
            u2 = [x + b[:, 0:LANES] for x, b in zip(u2, both)]
            power = [b[:, LANES:] for b in both]
        else:
            u2 = [x + mm(pw, x) for x, pw in zip(u2, power)]
    u = [x[0:ln] + x[ln:2 * ln] for x in u2]
    a_rb = [jnp.where(incl2, _dot_t(x, y.astype(BF16)), 0.0) for x, y in zip(xr, bd)]
    y = [s[ln:2 * ln] + merge(x[2 * ln:4 * ln]) - merge(mm(g, w)) for s, x, g, w in zip(sx, av, a_rb, u)]

    inv_n = 1.0 / half
    for p in pairs:
        sl = sls[p]
        w_rest = jnp.exp(tot[p] - cum[p])
        upd = mm(jnp.concatenate([v[p], u[p]], axis=0).T,
                 jnp.concatenate([k[p] * w_rest, -(bv[p] * w_rest)], axis=0))
        state[p // n_pairs, p % n_pairs] = s_old[p] * jnp.exp(tot[p]) + jnp.where(same_head, upd, 0.0)
        mean = _dot_sel(y[p], seg) * inv_n
        cen = y[p] - mean
        sums = _dot_sel(jnp.concatenate([cen * cen, r[p] * k[p] * rk_ref[:, sl]], axis=0), seg)
        var, bonus = sums[0:ln] * inv_n, sums[ln:2 * ln]
        y_ref[rws[p], sl] = cen * lax.rsqrt(var + A_GN_EPS) * lg_ref[:, sl] + lb_ref[:, sl] + bonus * v[p]

    @pl.when(c == pl.num_programs(1) - 1)
    def _():
        for b in range(bb):
            for p in range(n_pairs):
                sout_ref[b, 2 * p] = state[b, p, 0:half, 0:half]
                sout_ref[b, 2 * p + 1] = state[b, p, half:LANES, half:LANES]


def _rwkv_chunked(r, lw, k, v, kkr, a, s0, r_k, lnx_g, lnx_b, batch, seq, ln):
    nc = seq // ln
    n_pairs = A_WIDTH // LANES
    bb = RWKV_ROWS_PER_STEP if nc == 1 and batch % RWKV_ROWS_PER_STEP == 0 else 1
    row = pl.BlockSpec((bb * ln, A_WIDTH), lambda b, c: (b * nc + c, 0))
    st = pl.BlockSpec((bb, A_HEADS, A_HEAD_DIM, A_HEAD_DIM), lambda b, c: (b, 0, 0, 0))
    par = pl.BlockSpec((1, A_WIDTH), lambda b, c: (0, 0))
    return pl.pallas_call(
        functools.partial(_rwkv_chunk_kernel, ln=ln, bb=bb),
        grid=(batch // bb, nc),
        in_specs=[row] * 6 + [st] + [par] * 3,
        out_specs=[row, st],
        out_shape=[jax.ShapeDtypeStruct((batch * seq, A_WIDTH), F32),
                   jax.ShapeDtypeStruct((batch, A_HEADS, A_HEAD_DIM, A_HEAD_DIM), F32)],
        scratch_shapes=[pltpu.VMEM((bb, n_pairs, LANES, LANES), F32)],
        compiler_params=_cparams(2),
        name="rwkv_chunk",
    )(r, lw, k, v, kkr, a, s0, r_k.reshape(1, A_WIDTH), lnx_g.reshape(1, A_WIDTH), lnx_b.reshape(1, A_WIDTH))


def _rwkv_layer(x, batch, seq, s0, p_prev, prm, ln_g, ln_b, tm, tt_prep):
    (w_in, mu, w0, w2, a0, a2, k_k, k_a, r_k, lnx_g, lnx_b, w_out) = prm
    ln = min(RWKV_CHUNK, seq)
    assert seq % ln == 0 and ln % 8 == 0 and ln & (ln - 1) == 0
    zeros = jnp.zeros((A_LORA, A_WIDTH), F32)
    lora_w = jnp.concatenate([jnp.concatenate([w2, zeros], axis=1), jnp.concatenate([zeros, a2], axis=1)], axis=0)
    r, w, k, v, kkr, a, z, p_last = _rwkv_prep(x, w_in.astype(BF16), p_prev, mu, w0, a0, lora_w.astype(BF16), k_k,
                                               k_a, batch, seq, tt_prep, True)
    y, s_final = _rwkv_chunked(r, w, k, v, kkr, a, s0, r_k, lnx_g, lnx_b, batch, seq, ln)
    x_new = _outproj_ln(y, z, 0, x, w_out.astype(BF16), ln_g, ln_b, tm)
    return x_new, s_final, p_last.reshape(batch, A_NCOLS)


def _pool_kernel(u_ref, buf_ref, wg_ref, scale_ref, y_ref, nbuf_ref, ext, *, tt, front_valid):
    t = pl.program_id(1)

    @pl.when(t == 0)
    def _():
        ext[1:B_CARRY, :] = buf_ref[0]

    @pl.when(t > 0)
    def _():
        ext[0:B_CARRY, :] = ext[tt:tt + B_CARRY, :]

    u = u_ref[...]
    ext[B_CARRY:B_CARRY + tt, :] = u
    pos = t * tt + lax.broadcasted_iota(jnp.int32, (tt, 1), 0)
    for gi, win in enumerate(B_WINDOWS):
        lo, hi = gi * B_GROUP_W, (gi + 1) * B_GROUP_W
        ug = u[:, lo:hi]
        acc = ug
        for s in range(1, win):
            acc = acc + ext[B_CARRY - s:B_CARRY - s + tt, lo:hi]
        if front_valid:
            cnt = float(win)
        else:
            cnt = jnp.minimum(pos + 1, win).astype(F32)
        d = acc / cnt - ug
        yg = jnp.dot(d.astype(BF16), wg_ref[gi], preferred_element_type=F32)
        y_ref[:, lo:hi] = yg * scale_ref[:, lo:hi]

    @pl.when(t == pl.num_programs(1) - 1)
    def _():
        nbuf_ref[0] = ext[tt + 1:tt + B_CARRY, :]


def _pool(uz, buf, w_grp, scale, batch, seq, tt, front_valid):
    nt = seq // tt
    return pl.pallas_call(
        functools.partial(_pool_kernel, tt=tt, front_valid=front_valid),
        grid=(batch, nt),
        in_specs=[
            pl.BlockSpec((tt, B_WIDTH), lambda b, t: (b * nt + t, 0)),
            pl.BlockSpec((1, B_BUF, B_WIDTH), lambda b, t: (b, 0, 0)),
            pl.BlockSpec((len(B_WINDOWS), B_GROUP_W, B_GROUP_W), lambda b, t: (0, 0, 0)),
            pl.BlockSpec((1, B_WIDTH), lambda b, t: (0, 0)),
        ],
        out_specs=[
            pl.BlockSpec((tt, B_WIDTH), lambda b, t: (b * nt + t, 0)),
            pl.BlockSpec((1, B_BUF, B_WIDTH), lambda b, t: (b, 0, 0)),
        ],
        out_shape=[jax.ShapeDtypeStruct((batch * seq, B_WIDTH), F32),
                   jax.ShapeDtypeStruct((batch, B_BUF, B_WIDTH), F32)],
        scratch_shapes=[pltpu.VMEM((tt + B_CARRY, B_WIDTH), F32)],
        compiler_params=_cparams(2),
        name="pool",
    )(uz, buf, w_grp, scale.reshape(1, B_WIDTH))


def _pool_layer(x, batch, seq, buf, front_valid, prm, ln_g, ln_b, tm, tt):
    w_in, w_grp, scale, w_out = prm
    uz = _proj(x, w_in.astype(BF16), tm)
    y, new_buf = _pool(uz, buf, w_grp.astype(BF16), scale, batch, seq, tt, front_valid)
    x_new = _outproj_ln(y, uz, 1, x, w_out.astype(BF16), ln_g, ln_b, tm)
    return x_new, new_buf


NEG = -1e30
C_SCALE = C_HEAD_DIM ** -0.5


def _alibi_slopes():
    return jnp.power(2.0, -8.0 * (jnp.arange(C_HEADS, dtype=F32) + 1.0) / C_HEADS)


def _dot_t(a, b):
    return lax.dot_general(a, b, (((1,), (1,)), ((), ())), preferred_element_type=F32)


def _masked_softmax_parts(scores, masks, axis=1):
    top = None
    for s, k in zip(scores, masks):
        part = jnp.where(k, s, -jnp.inf)
        top = part if top is None else jnp.maximum(top, part)
    m = jnp.max(top, axis=axis, keepdims=True)
    m = jnp.where(jnp.isfinite(m), m, 0.0)
    es = [jnp.where(k, jnp.exp(s - m), 0.0) for s, k in zip(scores, masks)]
    total = es[0]
    for e in es[1:]:
        total = total + e
    den = jnp.maximum(jnp.sum(total, axis=axis, keepdims=True), 1e-30)
    return [e / den for e in es]


def _top_n_mask(imp, top_n):
    n = imp.shape[1]
    idx = lax.broadcasted_iota(jnp.int32, (1, n), 1)
    rank = jnp.zeros(imp.shape, F32)
    for c in range(n):
        col = imp[:, c:c + 1]
        rank = rank + jnp.where(idx > c, jnp.where(col >= imp, 1.0, 0.0), jnp.where(col > imp, 1.0, 0.0))
    return jnp.where(rank < top_n, jnp.where(imp >= 0.0, 1.0, 0.0), 0.0)


def _top_n_mask_wide(imp, top_n):
    n = imp.shape[1]
    shift = n.bit_length() - 1
    assert n == 1 << shift
    lane = lax.broadcasted_iota(jnp.int32, (n, n * n), 1)
    src = lax.broadcasted_iota(jnp.int32, (n, n * n), 0)
    from_cand = jnp.where(src == lax.shift_right_logical(lane, shift), 1.0, 0.0).astype(BF16)
    from_entry = jnp.where(src == (lane & (n - 1)), 1.0, 0.0).astype(BF16)
    cand = _dot_sel(imp, from_cand)
    entry = _dot_sel(imp, from_entry)
    c_idx = lax.shift_right_logical(lane[0:1], shift)
    s_idx = lane[0:1] & (n - 1)
    beats = jnp.where(s_idx > c_idx, jnp.where(cand >= entry, 1.0, 0.0), jnp.where(cand > entry, 1.0, 0.0))
    rank = _dot_t(beats.astype(BF16), from_entry)
    return jnp.where(rank < top_n, jnp.where(imp >= 0.0, 1.0, 0.0), 0.0)


def _block_expand(n_blocks, kpos):
    blk = lax.broadcasted_iota(jnp.int32, (n_blocks, kpos.shape[1]), 0)
    return jnp.where(lax.shift_right_logical(kpos, 6) == blk, 1.0, 0.0).astype(BF16)


def _cmp_kernel(kv_ref, w_ref, e_ref, o_ref):
    x = kv_ref[...]
    n_pair = x.shape[0] // (2 * C_CMP_BLOCK)
    x4 = x.reshape(n_pair, 2, C_CMP_BLOCK, x.shape[1])
    w = w_ref[...][None]
    e_ref[0] = jnp.sum(x4[:, 0] * w, axis=1)
    o_ref[0] = jnp.sum(x4[:, 1] * w, axis=1)


def _cmp(p, col_block, w, batch, seq):
    width = w.shape[1]
    n_pair = seq // (2 * C_CMP_BLOCK)
    out = jax.ShapeDtypeStruct((batch, n_pair, width), F32)
    return pl.pallas_call(
        _cmp_kernel,
        grid=(batch,),
        in_specs=[pl.BlockSpec((seq, width), lambda b: (b, col_block)), pl.BlockSpec(w.shape, lambda b: (0, 0))],
        out_specs=[pl.BlockSpec((1, n_pair, width), lambda b: (b, 0, 0))] * 2,
        out_shape=[out, out],
        compiler_params=_cparams(1),
        name="nsa_cmp",
    )(p, w)


def _nsa_prompt_kernel(q_ref, g_ref, ce_ref, co_ref, sel_ref, win_ref, slope_ref, o_ref, *, tq, tk):
    qi = pl.program_id(2)
    q0 = qi * tq
    hd = C_HEAD_DIM
    rows = C_HPG * tq
    n_pair = ce_ref.shape[1]
    stack = lambda f: jnp.concatenate([f(j) for j in range(C_HPG)], axis=0)
    tile4 = lambda x: jnp.concatenate([x] * C_HPG, axis=0)
    lo = lax.broadcasted_iota(jnp.int32, (1, LANES), 1) < hd
    q_all = q_ref[...]

    def q_head(j):
        pair_tile = q_all[:, (j // 2) * LANES:(j // 2 + 1) * LANES]
        return jnp.where(lo, pair_tile if j % 2 == 0 else pltpu.roll(pair_tile, hd, axis=1), 0.0)

    qb = (stack(q_head) * C_SCALE).astype(BF16)
    slope = stack(lambda j: jnp.broadcast_to(slope_ref[0, j:j + 1, 0:1], (tq, 1)))
    slope_keys = jnp.broadcast_to(slope, (rows, tk))
    qpos_t = q0 + lax.broadcasted_iota(jnp.int32, (tq, 1), 0)
    qpos = tile4(qpos_t)

    n_gate = 3 * C_HPG
    src = lax.broadcasted_iota(jnp.int32, (LANES, n_gate * LANES), 0)
    dst = lax.shift_right_logical(lax.broadcasted_iota(jnp.int32, (LANES, n_gate * LANES), 1), 7)
    gates = _dot_sel(_sigmoid(g_ref[...]), jnp.where(src == dst, 1.0, 0.0).astype(BF16))
    gate = lambda br: stack(lambda j: gates[:, (3 * j + br) * LANES:(3 * j + br + 1) * LANES])

    pair = lax.broadcasted_iota(jnp.int32, (1, n_pair), 1)
    ce, co = ce_ref[0].astype(BF16), co_ref[0].astype(BF16)
    scores, masks = [], []
    for par, c in ((0, ce), (1, co)):
        dist = qpos - ((2 * pair + par + 1) * C_CMP_BLOCK - 1)
        scores.append(_dot_t(qb, c) - slope * dist.astype(F32))
        masks.append(dist >= 0)
    p_e, p_o = _masked_softmax_parts(scores, masks)
    o_c = (jnp.dot(p_e.astype(BF16), ce, preferred_element_type=F32)
           + jnp.dot(p_o.astype(BF16), co, preferred_element_type=F32))
    imp_h = p_e + p_o
    imp = imp_h[0:tq]
    for j in range(1, C_HPG):
        imp = imp + imp_h[j * tq:(j + 1) * tq]
    cur = lax.shift_right_logical(qpos_t, 6)
    imp = jnp.where(pair == cur, C_FORCE, imp)
    imp = jnp.where(pair <= cur, imp, -1.0)
    sel = _top_n_mask_wide(imp, C_TOP_N).astype(BF16)

    def attend(carry, k0, kv, penalty_of):
        m, acc = carry
        kpos = k0 + lax.broadcasted_iota(jnp.int32, (1, tk), 1)
        dist = qpos_t - kpos
        s = (_dot_t(qb, kv.astype(BF16)) - slope_keys * tile4(dist.astype(F32))) + tile4(penalty_of(dist, kpos))
        m_new = jnp.maximum(m, jnp.max(s, axis=1, keepdims=True))
        p = jnp.exp(s - m_new).astype(BF16)
        ones_v = jnp.where(lo, 1.0, kv).astype(BF16)
        acc = jnp.exp(m - m_new) * acc + jnp.dot(p, ones_v, preferred_element_type=F32)
        return m_new, acc

    def finish(carry):
        _, acc = carry
        return acc / jnp.where(lo, 1.0, jnp.maximum(pltpu.roll(acc, hd, axis=1), 1e-30))

    init = (jnp.full((rows, 1), NEG, F32), jnp.zeros((rows, LANES), F32))

    def sel_step(c, carry):
        k0 = pl.multiple_of(c * tk, tk)

        def penalty(dist, kpos):
            chosen = jnp.dot(sel, _block_expand(n_pair, kpos), preferred_element_type=F32)
            return jnp.where(dist >= 0, jnp.where(chosen > 0.5, 0.0, NEG), NEG)

        return attend(carry, k0, sel_ref[pl.ds(k0, tk), :], penalty)

    o_s = finish(lax.fori_loop(0, qi, sel_step, sel_step(qi, init)))

    w0 = jnp.maximum(q0 - C_WINDOW, 0)
    n_win = (tq + C_WINDOW) // tk
    own = (q0 - w0) // tk

    def win_step(i, carry):
        k0 = pl.multiple_of(w0 + lax.rem(own + i, n_win) * tk, tk)
        penalty = lambda dist, kpos: jnp.where(dist >= 0, jnp.where(dist < C_WINDOW, 0.0, NEG), NEG)
        return attend(carry, k0, win_ref[pl.ds(k0, tk), :], penalty)

    o_w = finish(lax.fori_loop(1, n_win, win_step, win_step(0, init)))

    o = gate(0) * o_c + gate(1) * o_s + gate(2) * o_w
    head = lambda j: o[j * tq:(j + 1) * tq]
    o_ref[...] = jnp.concatenate([jnp.where(lo, pltpu.roll(head(2 * i), hd, axis=1), head(2 * i + 1))
                                  for i in range(C_HPG // 2)], axis=1)


def _nsa_prompt_layer(x, batch, seq, prm, ln_g, ln_b):
    w_in, cmp_wk, cmp_wv, w_out = prm
    hd, grp = C_HEAD_DIM, C_KV_HEADS
    tq = tk = 256
    assert seq % tq == 0 and seq >= tq + C_WINDOW and C_WINDOW % tk == 0 and seq % (2 * C_CMP_BLOCK) == 0
    assert tq == tk and seq == 2 * C_CMP_BLOCK * (seq // C_SEL_BLOCK)
    kv_w = lambda i: w_in[:, C_WIDTH + C_KV_WIDTH * i:C_WIDTH + C_KV_WIDTH * (i + 1)].reshape(D_MODEL, grp, hd)
    kv_pair = lambda a, b: jnp.concatenate([kv_w(a), kv_w(b)], axis=2).reshape(D_MODEL, grp * 2 * hd)
    g_lo = C_WIDTH + 6 * C_KV_WIDTH
    g_w = w_in[:, g_lo:g_lo + 3 * C_HEADS].reshape(D_MODEL, grp, 3 * C_HPG)
    g_w = jnp.pad(g_w, ((0, 0), (0, 0), (0, LANES - 3 * C_HPG))).reshape(D_MODEL, grp * LANES)
    w_p = jnp.concatenate([w_in[:, :C_WIDTH], w_in[:, g_lo + 3 * C_HEADS:], kv_pair(0, 1), kv_pair(2, 3),
                           kv_pair(4, 5), g_w, w_in[:, C_WIDTH:C_WIDTH + 4 * C_KV_WIDTH]], axis=1)
    n_main = 2 * C_WIDTH + 4 * grp * 2 * hd
    p, *new_rows = _proj(x, w_p.astype(BF16), 256, (n_main,) + (C_KV_WIDTH,) * 4)
    kv_lo = 2 * C_WIDTH
    kvw = grp * 2 * hd
    cw = jnp.concatenate([jnp.broadcast_to(cmp_wk[:, None], (C_CMP_BLOCK, hd)),
                          jnp.broadcast_to(cmp_wv[:, None], (C_CMP_BLOCK, hd))], axis=1)
    ce, co = _cmp(p, kv_lo // kvw, jnp.tile(cw, (1, grp)), batch, seq)
    slopes = jnp.broadcast_to(jnp.pad(_alibi_slopes().reshape(grp, C_HPG), ((0, 0), (0, 8 - C_HPG)))[:, :, None],
                              (grp, 8, LANES))
    nq = seq // tq
    n_pair = seq // (2 * C_CMP_BLOCK)
    gw = 2 * hd
    o = pl.pallas_call(
        functools.partial(_nsa_prompt_kernel, tq=tq, tk=tk),
        grid=(batch, grp, nq),
        in_specs=[
            pl.BlockSpec((tq, C_HPG * hd), lambda b, g, i: (b * nq + i, g)),
            pl.BlockSpec((tq, LANES), lambda b, g, i: (b * nq + i, (kv_lo + 3 * kvw) // LANES + g)),
            pl.BlockSpec((1, n_pair, gw), lambda b, g, i: (b, 0, g)),
            pl.BlockSpec((1, n_pair, gw), lambda b, g, i: (b, 0, g)),
            pl.BlockSpec((seq, gw), lambda b, g, i: (b, (kv_lo + kvw) // gw + g)),
            pl.BlockSpec((seq, gw), lambda b, g, i: (b, (kv_lo + 2 * kvw) // gw + g)),
            pl.BlockSpec((1, 8, LANES), lambda b, g, i: (g, 0, 0)),
        ],
        out_specs=pl.BlockSpec((tq, C_HPG * hd), lambda b, g, i: (b * nq + i, g)),
        out_shape=jax.ShapeDtypeStruct((batch * seq, C_WIDTH), F32),
        compiler_params=_cparams(3),
        name="nsa_prompt",
    )(p, p, ce, co, p, p, slopes)
    x_new = _outproj_ln(o, p, 1, x, w_out.astype(BF16), ln_g, ln_b, 512)
    rows = tuple(r.reshape(batch, seq, grp, hd) for r in new_rows)
    keep = min(C_WINDOW, seq)
    win = p.reshape(batch, seq, -1)[:, seq - keep:, kv_lo + 2 * kvw:kv_lo + 3 * kvw].reshape(batch, keep, grp, 2, hd)
    return x_new, rows, win[:, :, :, 0], win[:, :, :, 1]


def _page_cmp_kernel(tbl_ref, *refs, n_pages):
    del tbl_ref
    ck, cv = refs[:n_pages], refs[n_pages:2 * n_pages]
    wk_ref, wv_ref, tk_ref, tv_ref = refs[2 * n_pages:]
    per_page = PAGE_SIZE // C_CMP_BLOCK
    for pages, w_ref, t_ref in ((ck, wk_ref, tk_ref), (cv, wv_ref, tv_ref)):
        w = w_ref[...][None]
        for pg in range(n_pages):
            x = pages[pg][0].reshape(per_page, C_CMP_BLOCK, C_KV_HEADS, C_HEAD_DIM)
            t_ref[0, pg * per_page:(pg + 1) * per_page] = jnp.sum(x * w, axis=1)


def _nsa_sample_kernel(tbl_ref, *refs, n_pages, ts):
    del tbl_ref
    sk, sv = refs[:n_pages], refs[n_pages:2 * n_pages]
    (q_ref, g_ref, ksn_ref, vsn_ref, kwn_ref, vwn_ref, wk_ref, wv_ref, ke_ref, ko_ref, ve_ref, vo_ref, slope_ref,
     o_ref, new_s) = refs[2 * n_pages:]
    hd, grp = C_HEAD_DIM, C_KV_HEADS
    rows = C_HEADS * ts
    past = n_pages * PAGE_SIZE
    n_pair = past // C_SEL_BLOCK
    qpos = past + lax.rem(lax.broadcasted_iota(jnp.int32, (rows, 1), 0), ts)
    qb = (q_ref[0] * C_SCALE).astype(BF16)
    slope = slope_ref[:, 0:1]

    pair = lax.broadcasted_iota(jnp.int32, (1, n_pair), 1)
    scores, masks = [], []
    for par, k_ref in ((0, ke_ref), (1, ko_ref)):
        dist = qpos - ((2 * pair + par + 1) * C_CMP_BLOCK - 1)
        scores.append(_dot_t(qb, k_ref[0].astype(BF16)) - slope * dist.astype(F32))
        masks.append(dist >= 0)
    p_e, p_o = _masked_softmax_parts(scores, masks)
    o_c = (jnp.dot(p_e.astype(BF16), ve_ref[0].astype(BF16), preferred_element_type=F32)
           + jnp.dot(p_o.astype(BF16), vo_ref[0].astype(BF16), preferred_element_type=F32))
    imp_h = p_e + p_o
    sel_rows = []
    for g in range(grp):
        base = g * C_HPG * ts
        imp = imp_h[base:base + ts]
        for j in range(1, C_HPG):
            imp = imp + imp_h[base + j * ts:base + (j + 1) * ts]
        sel_g = _top_n_mask(imp, C_TOP_N - 1)
        sel_rows += [sel_g] * C_HPG
    sel_rows = jnp.concatenate(sel_rows, axis=0).astype(BF16)

    lane_pos = lax.broadcasted_iota(jnp.int32, (1, PAGE_SIZE), 1)

    def new_rows(slot, k_ref, v_ref):
        new_s[2 * slot:2 * slot + 2] = jnp.zeros((2,) + new_s.shape[1:], F32)
        new_s[2 * slot, 0:ts, :] = k_ref[...]
        new_s[2 * slot + 1, 0:ts, :] = v_ref[...]
        return new_s[2 * slot].astype(BF16), new_s[2 * slot + 1].astype(BF16)

    def softmax_attend(tiles):
        scores = []
        for kb, _, kpos, pen in tiles:
            dist = qpos - kpos
            scores.append((_dot_t(qb, kb) - slope * dist.astype(F32)) + pen(dist))
        top = scores[0]
        for s in scores[1:]:
            top = jnp.maximum(top, s)
        m = jnp.max(top, axis=1, keepdims=True)
        probs = [jnp.exp(s - m) for s in scores]
        total = probs[0]
        for p in probs[1:]:
            total = total + p
        acc = None
        for p, (_, vb, _, _) in zip(probs, tiles):
            part = jnp.dot(p.astype(BF16), vb, preferred_element_type=F32)
            acc = part if acc is None else acc + part
        return acc / jnp.maximum(jnp.sum(total, axis=1, keepdims=True), 1e-30)

    causal = lambda dist: jnp.where(dist >= 0, 0.0, NEG)
    window = lambda dist: jnp.where(dist >= 0, jnp.where(dist < C_WINDOW, 0.0, NEG), NEG)

    tiles = []
    for pg in range(n_pages):
        kpos = pg * PAGE_SIZE + lane_pos
        chosen = jnp.dot(sel_rows, _block_expand(n_pair, kpos), preferred_element_type=F32)
        pen = lambda dist, chosen=chosen: jnp.where(dist >= 0, jnp.where(chosen > 0.5, 0.0, NEG), NEG)
        tiles.append((sk[pg][0].astype(BF16), sv[pg][0].astype(BF16), kpos, pen))
    tiles.append(new_rows(0, ksn_ref, vsn_ref) + (past + lane_pos, causal))
    o_s = softmax_attend(tiles)

    n_win = wk_ref.shape[1]
    tiles = []
    for c in range(n_win // PAGE_SIZE):
        lo = c * PAGE_SIZE
        tiles.append((wk_ref[0, lo:lo + PAGE_SIZE, :].astype(BF16), wv_ref[0, lo:lo + PAGE_SIZE, :].astype(BF16),
                      past - n_win + lo + lane_pos, window))
    tiles.append(new_rows(1, kwn_ref, vwn_ref) + (past + lane_pos, window))
    o_w = softmax_attend(tiles)

    per_group = C_HPG * ts
    diag = lambda full: jnp.concatenate(
        [full[g * per_group:(g + 1) * per_group, g * hd:(g + 1) * hd] for g in range(grp)], axis=0)
    gates = _sigmoid(g_ref[0])
    o_ref[0] = gates[:, 0:1] * diag(o_c) + gates[:, 1:2] * diag(o_s) + gates[:, 2:3] * diag(o_w)


def _nsa_sample_layer(x, batch, ts, caches, page_table, win_k, win_v, prm, ln_g, ln_b):
    w_in, cmp_wk, cmp_wv, w_out = prm
    hd, grp = C_HEAD_DIM, C_KV_HEADS
    n_pages = page_table.shape[1]
    n_win = win_k.shape[1]
    past = n_pages * PAGE_SIZE
    rows = C_HEADS * ts
    assert past % C_SEL_BLOCK == 0 and ts <= C_SEL_BLOCK and ts % 8 == 0 and n_win == C_WINDOW and past >= n_win
    assert ts & (ts - 1) == 0 and ts <= 16 and rows == LANES
    g_lo = C_WIDTH + 6 * C_KV_WIDTH
    w_s = jnp.concatenate([w_in[:, :C_WIDTH], w_in[:, g_lo + 3 * C_HEADS:], w_in[:, C_WIDTH:g_lo],
                           jnp.pad(w_in[:, g_lo:g_lo + 3 * C_HEADS], ((0, 0), (0, LANES - 3 * C_HEADS)))], axis=1)
    p = _proj(x, w_s.astype(BF16), 512)
    kv_lo = 2 * C_WIDTH
    q = p[:, :C_WIDTH].reshape(batch, ts, grp, C_HPG, hd).transpose(0, 2, 3, 1, 4).reshape(batch, grp, C_HPG * ts, hd)
    q_bd = (q[:, :, :, None, :] * jnp.eye(grp, dtype=F32)[None, :, None, :, None]).reshape(batch, rows, grp * hd)
    g_lo_p = kv_lo + 6 * C_KV_WIDTH
    g_t = p[:, g_lo_p:g_lo_p + 3 * C_HEADS].reshape(batch, ts, C_HEADS, 3).transpose(0, 2, 1, 3).reshape(batch, rows, 3)
    slope_rows = jnp.broadcast_to(jnp.repeat(_alibi_slopes(), ts)[:, None], (rows, LANES))
    n_pair = past // C_SEL_BLOCK
    n_cmp = past // C_CMP_BLOCK
    row_w = lambda w: jnp.broadcast_to(w[:, None, None], (C_CMP_BLOCK, grp, hd))
    raw_page = lambda pg: pl.BlockSpec((1, PAGE_SIZE, grp, hd), lambda b, tbl: (tbl[b, pg], 0, 0, 0))
    tok_spec = pl.BlockSpec((1, n_cmp, grp, hd), lambda b, tbl: (b, 0, 0, 0))
    tok_shape = jax.ShapeDtypeStruct((batch, n_cmp, grp, hd), F32)
    tok_k, tok_v = pl.pallas_call(
        functools.partial(_page_cmp_kernel, n_pages=n_pages),
        grid_spec=pltpu.PrefetchScalarGridSpec(
            num_scalar_prefetch=1,
            grid=(batch,),
            in_specs=([raw_page(pg) for _ in range(2) for pg in range(n_pages)]
                      + [pl.BlockSpec((C_CMP_BLOCK, grp, hd), lambda b, tbl: (0, 0, 0))] * 2),
            out_specs=[tok_spec, tok_spec],
        ),
        out_shape=[tok_shape, tok_shape],
        compiler_params=_cparams(1),
        name="nsa_page_cmp",
    )(page_table, *[c for c in caches[:2] for _ in range(n_pages)], row_w(cmp_wk), row_w(cmp_wv))
    parity = lambda t, par: t.reshape(batch, n_pair, 2, grp * hd)[:, :, par]
    pools = [c.reshape(c.shape[0], PAGE_SIZE, grp * hd) for c in caches[2:]]
    page_spec = lambda pg: pl.BlockSpec((1, PAGE_SIZE, grp * hd), lambda b, tbl: (tbl[b, pg], 0, 0))
    new_spec = lambda i: pl.BlockSpec((ts, C_KV_WIDTH), lambda b, tbl: (b, kv_lo // C_KV_WIDTH + i))
    const2 = lambda shape: pl.BlockSpec(shape, lambda b, tbl: (0, 0))
    per_b = lambda shape: pl.BlockSpec((1,) + shape, lambda b, tbl: (b, 0, 0))
    in_specs = ([page_spec(pg) for _ in range(2) for pg in range(n_pages)]
                + [per_b((rows, grp * hd)), per_b((rows, 3)), new_spec(2), new_spec(3), new_spec(4), new_spec(5),
                   per_b((n_win, grp * hd)), per_b((n_win, grp * hd))] + [per_b((n_pair, grp * hd))] * 4
                + [const2((rows, LANES))])
    o = pl.pallas_call(
        functools.partial(_nsa_sample_kernel, n_pages=n_pages, ts=ts),
        grid_spec=pltpu.PrefetchScalarGridSpec(
            num_scalar_prefetch=1,
            grid=(batch,),
            in_specs=in_specs,
            out_specs=per_b((rows, hd)),
            scratch_shapes=[pltpu.VMEM((4, PAGE_SIZE, grp * hd), F32)],
        ),
        out_shape=jax.ShapeDtypeStruct((batch, rows, hd), F32),
        compiler_params=_cparams(1),
        name="nsa_sample",
    )(page_table, *[pool for pool in pools for _ in range(n_pages)], q_bd, g_t, p, p, p, p,
      win_k.reshape(batch, n_win, grp * hd), win_v.reshape(batch, n_win, grp * hd),
      parity(tok_k, 0), parity(tok_k, 1), parity(tok_v, 0), parity(tok_v, 1), slope_rows)
    new = lambda i: p[:, kv_lo + i * C_KV_WIDTH:kv_lo + (i + 1) * C_KV_WIDTH].reshape(batch, ts, grp, hd)
    y = o.reshape(batch, C_HEADS, ts, hd).transpose(0, 2, 1, 3).reshape(batch * ts, C_WIDTH)
    x_new = _outproj_ln(y, p, 1, x, w_out.astype(BF16), ln_g, ln_b, 512)
    keep = lambda buf, i: jnp.concatenate([buf, new(i)], axis=1)[:, -n_win:]
    return x_new, (new(0), new(1), new(2), new(3)), keep(win_k, 4), keep(win_v, 5)


def kernel(x_prompt, x_sample, state_rwkv_S, state_rwkv_shift, state_pool, cache_cmp_k, cache_cmp_v, cache_sel_k,
           cache_sel_v, state_win_k, state_win_v, page_table, ln_g, ln_b, a_w_in, a_mu, a_w0, a_w2, a_a0, a_a2, a_k_k,
           a_k_a, a_r_k, a_lnx_g, a_lnx_b, a_w_out, b_w_in, b_w_grp, b_scale, b_w_out, c_w_in, c_cmp_wk, c_cmp_wv,
           c_w_out):
    bp, tp, _ = x_prompt.shape
    bs, ts, _ = x_sample.shape
    xp = x_prompt.reshape(bp * tp, D_MODEL)
    xs = x_sample.reshape(bs * ts, D_MODEL)
    s_p, s_s, sh_p, sh_s, pl_p, pl_s = [], [], [], [], [], []
    rows_p, rows_s, wk_p, wk_s, wv_p, wv_s = [], [], [], [], [], []
    for layer in range(DEPTH):
        kind, li = layer % 3, layer // 3
        g, b = ln_g[layer], ln_b[layer]
        if kind == 0:
            prm = (a_w_in[li], a_mu[li], a_w0[li], a_w2[li], a_a0[li], a_a2[li], a_k_k[li], a_k_a[li], a_r_k[li],
                   a_lnx_g[li], a_lnx_b[li], a_w_out[li])
            xp, s_new, sh_new = _rwkv_layer(xp, bp, tp, jnp.zeros((bp, A_HEADS, A_HEAD_DIM, A_HEAD_DIM), F32),
                                            jnp.zeros((bp, A_NCOLS), F32), prm, g, b, 512, 256)
            s_p.append(s_new)
            sh_p.append(sh_new)
            xs, s_new, sh_new = _rwkv_layer(xs, bs, ts, state_rwkv_S[li], state_rwkv_shift[li], prm, g, b, 512, 64)
            s_s.append(s_new)
            sh_s.append(sh_new)
        elif kind == 1:
            prm = (b_w_in[li], b_w_grp[li], b_scale[li], b_w_out[li])
            xp, buf_new = _pool_layer(xp, bp, tp, jnp.zeros((bp, B_BUF, B_WIDTH), F32), False, prm, g, b, 512, 512)
            pl_p.append(buf_new)
            xs, buf_new = _pool_layer(xs, bs, ts, state_pool[li], True, prm, g, b, 512, ts)
            pl_s.append(buf_new)
        else:
            prm = (c_w_in[li], c_cmp_wk[li], c_cmp_wv[li], c_w_out[li])
            xp, rows, wk, wv = _nsa_prompt_layer(xp, bp, tp, prm, g, b)
            rows_p.append(rows)
            wk_p.append(wk)
            wv_p.append(wv)
            caches = (cache_cmp_k[li], cache_cmp_v[li], cache_sel_k[li], cache_sel_v[li])
            xs, rows, wk, wv = _nsa_sample_layer(xs, bs, ts, caches, page_table, state_win_k[li], state_win_v[li],
                                                 prm, g, b)
            rows_s.append(rows)
            wk_s.append(wk)
            wv_s.append(wv)
    stack = jnp.stack
    return (xp.reshape(bp, tp, D_MODEL), xs.reshape(bs, ts, D_MODEL), stack(s_p), stack(s_s), stack(sh_p), stack(sh_s),
            stack(pl_p), stack(pl_s),
            stack([r[0] for r in rows_p]), stack([r[0] for r in rows_s]),
            stack([r[1] for r in rows_p]), stack([r[1] for r in rows_s]),
            stack([r[2] for r in rows_p]), stack([r[2] for r in rows_s]),
            stack([r[3] for r in rows_p]), stack([r[3] for r in rows_s]),
            stack(wk_p), stack(wk_s), stack(wv_p), stack(wv_s))
```

```python
import functools

import jax
import jax.numpy as jnp
import numpy as np
from jax import lax
from jax.experimental import pallas as pl
from jax.experimental.pallas import tpu as pltpu

F32 = jnp.float32
BF16 = jnp.bfloat16

D_MODEL = 1024
DEPTH = 4
DEEPNORM_ALPHA = (2.0 * DEPTH) ** 0.25
LN_EPS = 1e-5

A_HEADS = 16
A_HEAD_DIM = 64
A_WIDTH = 1024
A_LORA = 64
A_NCOLS = 4 * A_WIDTH + 2 * A_LORA
A_GN_EPS = 64e-5
RWKV_CHUNK = 64
RWKV_ROWS_PER_STEP = 4

B_WIDTH = 1024
B_GROUP_W = 256
B_WINDOWS = (2, 4, 8, 16)
B_BUF = 15
B_CARRY = 16

C_HEADS = 16
C_KV_HEADS = 4
C_HPG = 4
C_HEAD_DIM = 64
C_WIDTH = 1024
C_KV_WIDTH = 256
C_CMP_BLOCK = 32
C_SEL_BLOCK = 64
C_TOP_N = 16
C_WINDOW = 512
C_FORCE = 1e9
PAGE_SIZE = 128

LANES = 128
VMEM_LIMIT = 56 * 1024 * 1024


def _cparams(n_axes):
    return pltpu.CompilerParams(dimension_semantics=("arbitrary",) * n_axes, vmem_limit_bytes=VMEM_LIMIT)


def _sigmoid(x):
    return 1.0 / (1.0 + jnp.exp(-x))


def _silu(x):
    return x * _sigmoid(x)


def _proj_kernel(x_ref, w_ref, *o_refs):
    acc = jnp.dot(x_ref[...].astype(BF16), w_ref[...], preferred_element_type=F32)
    lo = 0
    for o_ref in o_refs:
        width = o_ref.shape[1]
        o_ref[...] = acc[:, lo:lo + width]
        lo += width


def _proj(x, w, tm, widths=None):
    m, k = x.shape
    n = w.shape[1]
    widths = (n,) if widths is None else widths
    assert sum(widths) == n and all(wd % LANES == 0 for wd in widths)
    outs = pl.pallas_call(
        _proj_kernel,
        grid=(m // tm,),
        in_specs=[pl.BlockSpec((tm, k), lambda i: (i, 0)), pl.BlockSpec((k, n), lambda i: (0, 0))],
        out_specs=[pl.BlockSpec((tm, wd), lambda i: (i, 0)) for wd in widths],
        out_shape=[jax.ShapeDtypeStruct((m, wd), F32) for wd in widths],
        compiler_params=_cparams(1),
        name="proj",
    )(x, w)
    return outs[0] if len(widths) == 1 else outs


def _outproj_ln_kernel(y_ref, z_ref, x_ref, w_ref, g_ref, b_ref, o_ref):
    a = (y_ref[...] * _silu(z_ref[...])).astype(BF16)
    h = DEEPNORM_ALPHA * x_ref[...] + jnp.dot(a, w_ref[...], preferred_element_type=F32)
    mu = jnp.mean(h, axis=-1, keepdims=True)
    c = h - mu
    var = jnp.mean(c * c, axis=-1, keepdims=True)
    o_ref[...] = c * lax.rsqrt(var + LN_EPS) * g_ref[...] + b_ref[...]


def _outproj_ln(y, z_arr, z_col, x, w, g, b, tm):
    m = x.shape[0]
    row = lambda i: (i, 0)
    const = lambda i: (0, 0)
    return pl.pallas_call(
        _outproj_ln_kernel,
        grid=(m // tm,),
        in_specs=[
            pl.BlockSpec((tm, D_MODEL), row),
            pl.BlockSpec((tm, D_MODEL), lambda i: (i, z_col)),
            pl.BlockSpec((tm, D_MODEL), row),
            pl.BlockSpec((D_MODEL, D_MODEL), const),
            pl.BlockSpec((1, D_MODEL), const),
            pl.BlockSpec((1, D_MODEL), const),
        ],
        out_specs=pl.BlockSpec((tm, D_MODEL), row),
        out_shape=jax.ShapeDtypeStruct((m, D_MODEL), F32),
        compiler_params=_cparams(1),
        name="outproj_ln",
    )(y, z_arr, x, w, g.reshape(1, D_MODEL), b.reshape(1, D_MODEL))


def _rwkv_prep_kernel(x_ref, win_ref, prev_ref, mu_ref, w0_ref, a0_ref, lora_ref, kk_ref, ka_ref,
                      r_o, w_o, k_o, v_o, kk_o, a_o, z_o, last_o, carry, *, tt, nb, log_decay):
    p = jnp.dot(x_ref[...].astype(BF16), win_ref[...], preferred_element_type=F32)
    seq = tt // nb
    for i in range(nb):
        last_o[i] = p[(i + 1) * seq - 1:(i + 1) * seq, :]
    row = lax.broadcasted_iota(jnp.int32, (tt, 1), 0)
    if nb == 1:
        @pl.when(pl.program_id(1) == 0)
        def _():
            carry[0:1, :] = prev_ref[0]

        before = carry[0:1, :]
        first = row == 0
    else:
        before = jnp.concatenate([jnp.broadcast_to(prev_ref[i], (seq, A_NCOLS)) for i in range(nb)], axis=0)
        first = (row & (seq - 1)) == 0
    p_shift = jnp.where(first, before, pltpu.roll(p, 1, axis=0))
    carry[0:1, :] = p[tt - 1:tt, :]
    pm = p + (p_shift - p) * mu_ref[...]
    r = pm[:, 0:A_WIDTH]
    k = pm[:, A_WIDTH:2 * A_WIDTH]
    v = pm[:, 2 * A_WIDTH:3 * A_WIDTH]
    z = pm[:, 3 * A_WIDTH:4 * A_WIDTH]
    lo = pm[:, 4 * A_WIDTH:A_NCOLS]
    is_w = lax.broadcasted_iota(jnp.int32, (1, 2 * A_LORA), 1) < A_LORA
    lo = jnp.where(is_w, jnp.tanh(lo), lo)
    lora = jnp.dot(lo.astype(BF16), lora_ref[...], preferred_element_type=F32)
    u = w0_ref[...] + lora[:, 0:A_WIDTH]
    w_log = jnp.minimum(u, 0.0) - jnp.log(1.0 + jnp.exp(-jnp.abs(u))) - 0.5
    a = _sigmoid(a0_ref[...] + lora[:, A_WIDTH:2 * A_WIDTH])
    r_o[...] = r
    w_o[...] = -jnp.exp(w_log) if log_decay else jnp.exp(-jnp.exp(w_log))
    k_o[...] = k * (1.0 + (a - 1.0) * ka_ref[...])
    v_o[...] = v
    kk_o[...] = k * kk_ref[...]
    a_o[...] = a
    z_o[...] = z


def _rwkv_prep(x, w_in, p_prev, mu, w0, a0, lora_w, k_k, k_a, batch, seq, tt, log_decay):
    nb = max(1, tt // seq)
    assert (seq % tt == 0 and nb == 1) or (tt % seq == 0 and batch % nb == 0 and seq & (seq - 1) == 0)
    nt = max(1, seq // tt)
    row = lambda b, t: (b * nt + t, 0)
    const = lambda b, t: (0, 0)
    vec = lambda a: a.reshape(1, -1)
    out = jax.ShapeDtypeStruct((batch * seq, A_WIDTH), F32)
    per_seq = pl.BlockSpec((nb, 1, A_NCOLS), lambda b, t: (b, 0, 0))
    return pl.pallas_call(
        functools.partial(_rwkv_prep_kernel, tt=tt, nb=nb, log_decay=log_decay),
        grid=(batch // nb, nt),
        in_specs=[
            pl.BlockSpec((tt, D_MODEL), row),
            pl.BlockSpec((D_MODEL, A_NCOLS), const),
            per_seq,
            pl.BlockSpec((1, A_NCOLS), const),
            pl.BlockSpec((1, A_WIDTH), const),
            pl.BlockSpec((1, A_WIDTH), const),
            pl.BlockSpec((2 * A_LORA, 2 * A_WIDTH), const),
            pl.BlockSpec((1, A_WIDTH), const),
            pl.BlockSpec((1, A_WIDTH), const),
        ],
        out_specs=[pl.BlockSpec((tt, A_WIDTH), row)] * 7 + [per_seq],
        out_shape=[out] * 7 + [jax.ShapeDtypeStruct((batch, 1, A_NCOLS), F32)],
        scratch_shapes=[pltpu.VMEM((8, A_NCOLS), F32)],
        compiler_params=_cparams(2),
        name="rwkv_prep",
    )(x, w_in, p_prev.reshape(batch, 1, A_NCOLS), vec(mu), vec(w0), vec(a0), lora_w, vec(k_k), vec(k_a))


def _split3(x):
    hi = x.astype(BF16)
    rest = x - hi.astype(F32)
    mid = rest.astype(BF16)
    return hi, mid, (rest - mid.astype(F32)).astype(BF16)


def _dot_sel(x, sel, pieces=3):
    hi, mid, lo = _split3(x)
    n = x.shape[0]
    if pieces == 2:
        if n % 16 == 0:
            out = jnp.dot(jnp.concatenate([hi, mid], axis=0), sel, preferred_element_type=F32)
            return out[0:n] + out[n:2 * n]
        return jnp.dot(hi, sel, preferred_element_type=F32) + jnp.dot(mid, sel, preferred_element_type=F32)
    if n % 16 == 0:
        out = jnp.dot(jnp.concatenate([hi, mid, lo], axis=0), sel, preferred_element_type=F32)
        return out[0:n] + (out[n:2 * n] + out[2 * n:3 * n])
    d = lambda a: jnp.dot(a, sel, preferred_element_type=F32)
    return d(hi) + (d(mid) + d(lo))


def _rwkv_chunk_kernel(r_ref, lw_ref, k_ref, v_ref, kkr_ref, a_ref, s0_ref, rk_ref, lg_ref, lb_ref,
                       y_ref, sout_ref, state, *, ln, bb):
    c = pl.program_id(1)
    half = A_HEAD_DIM
    n_pairs = A_WIDTH // LANES

    @pl.when(c == 0)
    def _():
        zero = jnp.zeros((half, half), F32)
        for b in range(bb):
            for p in range(n_pairs):
                top = jnp.concatenate([s0_ref[b, 2 * p], zero], axis=1)
                bottom = jnp.concatenate([zero, s0_ref[b, 2 * p + 1]], axis=1)
                state[b, p] = jnp.concatenate([top, bottom], axis=0)

    row = lax.broadcasted_iota(jnp.int32, (ln, 1), 0)
    lo = lax.broadcasted_iota(jnp.int32, (1, LANES), 1) < half
    r2 = lax.broadcasted_iota(jnp.int32, (2 * ln, ln), 0) & (ln - 1)
    c2 = lax.broadcasted_iota(jnp.int32, (2 * ln, ln), 1)
    strict2, incl2 = c2 < r2, c2 <= r2
    rr = lax.broadcasted_iota(jnp.int32, (LANES, LANES), 0)
    cc = lax.broadcasted_iota(jnp.int32, (LANES, LANES), 1)
    same_head = (rr & half) == (cc & half)
    seg = jnp.where(same_head, 1.0, 0.0).astype(BF16)
    rb = lax.broadcasted_iota(jnp.int32, (2 * ln, 2 * ln), 0)
    cb = lax.broadcasted_iota(jnp.int32, (2 * ln, 2 * ln), 1)
    strict_bd = ((cb & (ln - 1)) + jnp.where((rb & ln) == (cb & ln), 0, ln)) < (rb & (ln - 1))
    split = lambda x: jnp.concatenate([jnp.where(lo, x, 0.0), jnp.where(lo, 0.0, x)], axis=0)
    merge = lambda x2: jnp.where(lo, x2[0:ln], x2[ln:2 * ln])
    mm = lambda a, b: jnp.dot(a.astype(BF16), b.astype(BF16), preferred_element_type=F32)

    pairs = range(bb * n_pairs)
    sls = [slice((p % n_pairs) * LANES, (p % n_pairs + 1) * LANES) for p in pairs]
    rws = [slice((p // n_pairs) * ln, (p // n_pairs + 1) * ln) for p in pairs]
    load = lambda ref: [ref[rw, sl] for rw, sl in zip(rws, sls)]
    r, lw, k, v, kkr, a = (load(ref) for ref in (r_ref, lw_ref, k_ref, v_ref, kkr_ref, a_ref))
    norm2 = [_dot_sel(x * x, seg, pieces=2) for x in kkr]
    kk = [x * lax.rsqrt(jnp.maximum(n2, 1e-24)) for x, n2 in zip(kkr, norm2)]
    bv = [x * y for x, y in zip(kk, a)]
    cum = lw
    d = 1
    while d < ln:
        cum = [x + jnp.where(row >= d, pltpu.roll(x, d, axis=0), 0.0) for x in cum]
        d *= 2
    tot = [x[ln - 1:ln, :] for x in cum]
    kkd = [x * jnp.exp(cm - l) for x, cm, l in zip(kk, cum, lw)]
    rd = [x * jnp.exp(cm) for x, cm in zip(r, cum)]
    w_inv = [jnp.exp(-cm) for cm in cum]
    kd = [x * w for x, w in zip(k, w_inv)]
    bd = [x * w for x, w in zip(bv, w_inv)]
    s_old = [state[p // n_pairs, p % n_pairs] for p in pairs]

    xk = [split(x).astype(BF16) for x in kkd]
    xr = [split(x).astype(BF16) for x in rd]
    nil = [jnp.where(strict_bd, -_dot_t(x, split(y).astype(BF16)), 0.0) for x, y in zip(xk, bd)]
    gk = [_dot_t(jnp.concatenate([x, y], axis=0), z.astype(BF16)) for x, y, z in zip(xk, xr, kd)]
    sx = [_dot_t(jnp.concatenate([x, y], axis=0).astype(BF16), s.astype(BF16)) for x, y, s in zip(kkd, rd, s_old)]
    av = [mm(jnp.concatenate([jnp.where(strict2, g[0:2 * ln], 0.0), jnp.where(incl2, g[2 * ln:4 * ln], 0.0)], axis=0), x)
          for g, x in zip(gk, v)]
    u2 = [split(s[0:ln] + merge(x[0:2 * ln])) for s, x in zip(sx, av)]
    power = nil
    step = 1
    while step < ln:
        step *= 2
        if step < ln:
            both = [mm(pw, jnp.concatenate([x, pw], axis=1)) for x, pw in zip(u2, power)]
            u2 = [x + b[:, 0:LANES] for x, b in zip(u2, both)]
            power = [b[:, LANES:] for b in both]
        else:
            u2 = [x + mm(pw, x) for x, pw in zip(u2, power)]
    u = [x[0:ln] + x[ln:2 * ln] for x in u2]
    a_rb = [jnp.where(incl2, _dot_t(x, y.astype(BF16)), 0.0) for x, y in zip(xr, bd)]
    y = [s[ln:2 * ln] + merge(x[2 * ln:4 * ln]) - merge(mm(g, w)) for s, x, g, w in zip(sx, av, a_rb, u)]

    inv_n = 1.0 / half
    for p in pairs:
        sl = sls[p]
        w_rest = jnp.exp(tot[p] - cum[p])
        upd = mm(jnp.concatenate([v[p], u[p]], axis=0).T,
                 jnp.concatenate([k[p] * w_rest, -(bv[p] * w_rest)], axis=0))
        state[p // n_pairs, p % n_pairs] = s_old[p] * jnp.exp(tot[p]) + jnp.where(same_head, upd, 0.0)
        mean = _dot_sel(y[p], seg, pieces=2) * inv_n
        cen = y[p] - mean
        sums = _dot_sel(jnp.concatenate([cen * cen, r[p] * k[p] * rk_ref[:, sl]], axis=0), seg, pieces=2)
        var, bonus = sums[0:ln] * inv_n, sums[ln:2 * ln]
        y_ref[rws[p], sl] = cen * lax.rsqrt(var + A_GN_EPS) * lg_ref[:, sl] + lb_ref[:, sl] + bonus * v[p]

    @pl.when(c == pl.num_programs(1) - 1)
    def _():
        for b in range(bb):
            for p in range(n_pairs):
                sout_ref[b, 2 * p] = state[b, p, 0:half, 0:half]
                sout_ref[b, 2 * p + 1] = state[b, p, half:LANES, half:LANES]


def _rwkv_chunked(r, lw, k, v, kkr, a, s0, r_k, lnx_g, lnx_b, batch, seq, ln):
    nc = seq // ln
    n_pairs = A_WIDTH // LANES
    bb = RWKV_ROWS_PER_STEP if nc == 1 and batch % RWKV_ROWS_PER_STEP == 0 else 1
    row = pl.BlockSpec((bb * ln, A_WIDTH), lambda b, c: (b * nc + c, 0))
    st = pl.BlockSpec((bb, A_HEADS, A_HEAD_DIM, A_HEAD_DIM), lambda b, c: (b, 0, 0, 0))
    par = pl.BlockSpec((1, A_WIDTH), lambda b, c: (0, 0))
    return pl.pallas_call(
        functools.partial(_rwkv_chunk_kernel, ln=ln, bb=bb),
        grid=(batch // bb, nc),
        in_specs=[row] * 6 + [st] + [par] * 3,
        out_specs=[row, st],
        out_shape=[jax.ShapeDtypeStruct((batch * seq, A_WIDTH), F32),
                   jax.ShapeDtypeStruct((batch, A_HEADS, A_HEAD_DIM, A_HEAD_DIM), F32)],
        scratch_shapes=[pltpu.VMEM((bb, n_pairs, LANES, LANES), F32)],
        compiler_params=_cparams(2),
        name="rwkv_chunk",
    )(r, lw, k, v, kkr, a, s0, r_k.reshape(1, A_WIDTH), lnx_g.reshape(1, A_WIDTH), lnx_b.reshape(1, A_WIDTH))


def _rwkv_layer(x, batch, seq, s0, p_prev, prm, ln_g, ln_b, tm, tt_prep):
    (w_in, mu, w0, w2, a0, a2, k_k, k_a, r_k, lnx_g, lnx_b, w_out) = prm
    ln = min(RWKV_CHUNK, seq)
    assert seq % ln == 0 and ln % 8 == 0 and ln & (ln - 1) == 0
    zeros = jnp.zeros((A_LORA, A_WIDTH), F32)
    lora_w = jnp.concatenate([jnp.concatenate([w2, zeros], axis=1), jnp.concatenate([zeros, a2], axis=1)], axis=0)
    r, w, k, v, kkr, a, z, p_last = _rwkv_prep(x, w_in.astype(BF16), p_prev, mu, w0, a0, lora_w.astype(BF16), k_k,
                                               k_a, batch, seq, tt_prep, True)
    y, s_final = _rwkv_chunked(r, w, k, v, kkr, a, s0, r_k, lnx_g, lnx_b, batch, seq, ln)
    x_new = _outproj_ln(y, z, 0, x, w_out.astype(BF16), ln_g, ln_b, tm)
    return x_new, s_final, p_last.reshape(batch, A_NCOLS)


def _pool_kernel(u_ref, buf_ref, wg_ref, scale_ref, y_ref, nbuf_ref, ext, *, tt, front_valid):
    t = pl.program_id(1)

    @pl.when(t == 0)
    def _():
        ext[1:B_CARRY, :] = buf_ref[0]

    @pl.when(t > 0)
    def _():
        ext[0:B_CARRY, :] = ext[tt:tt + B_CARRY, :]

    u = u_ref[...]
    ext[B_CARRY:B_CARRY + tt, :] = u
    pos = t * tt + lax.broadcasted_iota(jnp.int32, (tt, 1), 0)
    for gi, win in enumerate(B_WINDOWS):
        lo, hi = gi * B_GROUP_W, (gi + 1) * B_GROUP_W
        ug = u[:, lo:hi]
        acc = ug
        for s in range(1, win):
            acc = acc + ext[B_CARRY - s:B_CARRY - s + tt, lo:hi]
        if front_valid:
            cnt = float(win)
        else:
            cnt = jnp.minimum(pos + 1, win).astype(F32)
        d = acc / cnt - ug
        yg = jnp.dot(d.astype(BF16), wg_ref[gi], preferred_element_type=F32)
        y_ref[:, lo:hi] = yg * scale_ref[:, lo:hi]

    @pl.when(t == pl.num_programs(1) - 1)
    def _():
        nbuf_ref[0] = ext[tt + 1:tt + B_CARRY, :]


def _pool(uz, buf, w_grp, scale, batch, seq, tt, front_valid):
    nt = seq // tt
    return pl.pallas_call(
        functools.partial(_pool_kernel, tt=tt, front_valid=front_valid),
        grid=(batch, nt),
        in_specs=[
            pl.BlockSpec((tt, B_WIDTH), lambda b, t: (b * nt + t, 0)),
            pl.BlockSpec((1, B_BUF, B_WIDTH), lambda b, t: (b, 0, 0)),
            pl.BlockSpec((len(B_WINDOWS), B_GROUP_W, B_GROUP_W), lambda b, t: (0, 0, 0)),
            pl.BlockSpec((1, B_WIDTH), lambda b, t: (0, 0)),
        ],
        out_specs=[
            pl.BlockSpec((tt, B_WIDTH), lambda b, t: (b * nt + t, 0)),
            pl.BlockSpec((1, B_BUF, B_WIDTH), lambda b, t: (b, 0, 0)),
        ],
        out_shape=[jax.ShapeDtypeStruct((batch * seq, B_WIDTH), F32),
                   jax.ShapeDtypeStruct((batch, B_BUF, B_WIDTH), F32)],
        scratch_shapes=[pltpu.VMEM((tt + B_CARRY, B_WIDTH), F32)],
        compiler_params=_cparams(2),
        name="pool",
    )(uz, buf, w_grp, scale.reshape(1, B_WIDTH))


def _pool_layer(x, batch, seq, buf, front_valid, prm, ln_g, ln_b, tm, tt):
    w_in, w_grp, scale, w_out = prm
    uz = _proj(x, w_in.astype(BF16), tm)
    y, new_buf = _pool(uz, buf, w_grp.astype(BF16), scale, batch, seq, tt, front_valid)
    x_new = _outproj_ln(y, uz, 1, x, w_out.astype(BF16), ln_g, ln_b, tm)
    return x_new, new_buf


NEG = -1e30
C_SCALE = C_HEAD_DIM ** -0.5


def _alibi_slopes():
    return jnp.power(2.0, -8.0 * (jnp.arange(C_HEADS, dtype=F32) + 1.0) / C_HEADS)


def _dot_t(a, b):
    return lax.dot_general(a, b, (((1,), (1,)), ((), ())), preferred_element_type=F32)


def _masked_softmax_parts(scores, masks, axis=1):
    top = None
    for s, k in zip(scores, masks):
        part = jnp.where(k, s, -jnp.inf)
        top = part if top is None else jnp.maximum(top, part)
    m = jnp.max(top, axis=axis, keepdims=True)
    m = jnp.where(jnp.isfinite(m), m, 0.0)
    es = [jnp.where(k, jnp.exp(s - m), 0.0) for s, k in zip(scores, masks)]
    total = es[0]
    for e in es[1:]:
        total = total + e
    den = jnp.maximum(jnp.sum(total, axis=axis, keepdims=True), 1e-30)
    return [e / den for e in es]


def _top_n_mask(imp, top_n):
    n = imp.shape[1]
    idx = lax.broadcasted_iota(jnp.int32, (1, n), 1)
    rank = jnp.zeros(imp.shape, F32)
    for c in range(n):
        col = imp[:, c:c + 1]
        rank = rank + jnp.where(idx > c, jnp.where(col >= imp, 1.0, 0.0), jnp.where(col > imp, 1.0, 0.0))
    return jnp.where(rank < top_n, jnp.where(imp >= 0.0, 1.0, 0.0), 0.0)


def _top_n_mask_wide(imp, top_n):
    n = imp.shape[1]
    shift = n.bit_length() - 1
    assert n == 1 << shift
    lane = lax.broadcasted_iota(jnp.int32, (n, n * n), 1)
    src = lax.broadcasted_iota(jnp.int32, (n, n * n), 0)
    from_cand = jnp.where(src == lax.shift_right_logical(lane, shift), 1.0, 0.0).astype(BF16)
    from_entry = jnp.where(src == (lane & (n - 1)), 1.0, 0.0).astype(BF16)
    cand = _dot_sel(imp, from_cand)
    entry = _dot_sel(imp, from_entry)
    c_idx = lax.shift_right_logical(lane[0:1], shift)
    s_idx = lane[0:1] & (n - 1)
    beats = jnp.where(s_idx > c_idx, jnp.where(cand >= entry, 1.0, 0.0), jnp.where(cand > entry, 1.0, 0.0))
    rank = _dot_t(beats.astype(BF16), from_entry)
    return jnp.where(rank < top_n, jnp.where(imp >= 0.0, 1.0, 0.0), 0.0)


def _block_expand(n_blocks, kpos):
    blk = lax.broadcasted_iota(jnp.int32, (n_blocks, kpos.shape[1]), 0)
    return jnp.where(lax.shift_right_logical(kpos, 6) == blk, 1.0, 0.0).astype(BF16)


def _cmp_kernel(kv_ref, w_ref, e_ref, o_ref):
    x = kv_ref[...]
    n_pair = x.shape[0] // (2 * C_CMP_BLOCK)
    x4 = x.reshape(n_pair, 2, C_CMP_BLOCK, x.shape[1])
    w = w_ref[...][None]
    e_ref[0] = jnp.sum(x4[:, 0] * w, axis=1)
    o_ref[0] = jnp.sum(x4[:, 1] * w, axis=1)


def _cmp(p, col_block, w, batch, seq):
    width = w.shape[1]
    n_pair = seq // (2 * C_CMP_BLOCK)
    out = jax.ShapeDtypeStruct((batch, n_pair, width), F32)
    return pl.pallas_call(
        _cmp_kernel,
        grid=(batch,),
        in_specs=[pl.BlockSpec((seq, width), lambda b: (b, col_block)), pl.BlockSpec(w.shape, lambda b: (0, 0))],
        out_specs=[pl.BlockSpec((1, n_pair, width), lambda b: (b, 0, 0))] * 2,
        out_shape=[out, out],
        compiler_params=_cparams(1),
        name="nsa_cmp",
    )(p, w)


def _nsa_prompt_kernel(q_ref, g_ref, ce_ref, co_ref, sel_ref, win_ref, slope_ref, o_ref, *, tq, tk):
    qi = pl.program_id(2)
    q0 = qi * tq
    hd = C_HEAD_DIM
    rows = C_HPG * tq
    n_pair = ce_ref.shape[1]
    stack = lambda f: jnp.concatenate([f(j) for j in range(C_HPG)], axis=0)
    tile4 = lambda x: jnp.concatenate([x] * C_HPG, axis=0)
    lo = lax.broadcasted_iota(jnp.int32, (1, LANES), 1) < hd
    q_all = q_ref[...]

    def q_head(j):
        pair_tile = q_all[:, (j // 2) * LANES:(j // 2 + 1) * LANES]
        return jnp.where(lo, pair_tile if j % 2 == 0 else pltpu.roll(pair_tile, hd, axis=1), 0.0)

    qb = (stack(q_head) * C_SCALE).astype(BF16)
    slope = stack(lambda j: jnp.broadcast_to(slope_ref[0, j:j + 1, 0:1], (tq, 1)))
    slope_keys = jnp.broadcast_to(slope, (rows, tk))
    qpos_t = q0 + lax.broadcasted_iota(jnp.int32, (tq, 1), 0)
    qpos = tile4(qpos_t)

    n_gate = 3 * C_HPG
    src = lax.broadcasted_iota(jnp.int32, (LANES, n_gate * LANES), 0)
    dst = lax.shift_right_logical(lax.broadcasted_iota(jnp.int32, (LANES, n_gate * LANES), 1), 7)
    gates = _dot_sel(_sigmoid(g_ref[...]), jnp.where(src == dst, 1.0, 0.0).astype(BF16))
    gate = lambda br: stack(lambda j: gates[:, (3 * j + br) * LANES:(3 * j + br + 1) * LANES])

    pair = lax.broadcasted_iota(jnp.int32, (1, n_pair), 1)
    ce, co = ce_ref[0].astype(BF16), co_ref[0].astype(BF16)
    scores, masks = [], []
    for par, c in ((0, ce), (1, co)):
        dist = qpos - ((2 * pair + par + 1) * C_CMP_BLOCK - 1)
        scores.append(_dot_t(qb, c) - slope * dist.astype(F32))
        masks.append(dist >= 0)
    p_e, p_o = _masked_softmax_parts(scores, masks)
    o_c = (jnp.dot(p_e.astype(BF16), ce, preferred_element_type=F32)
           + jnp.dot(p_o.astype(BF16), co, preferred_element_type=F32))
    imp_h = p_e + p_o
    imp = imp_h[0:tq]
    for j in range(1, C_HPG):
        imp = imp + imp_h[j * tq:(j + 1) * tq]
    cur = lax.shift_right_logical(qpos_t, 6)
    imp = jnp.where(pair == cur, C_FORCE, imp)
    imp = jnp.where(pair <= cur, imp, -1.0)
    sel = _top_n_mask_wide(imp, C_TOP_N).astype(BF16)

    def attend(carry, k0, kv, penalty_of):
        m, acc = carry
        kpos = k0 + lax.broadcasted_iota(jnp.int32, (1, tk), 1)
        dist = qpos_t - kpos
        s = (_dot_t(qb, kv.astype(BF16)) - slope_keys * tile4(dist.astype(F32))) + tile4(penalty_of(dist, kpos))
        m_new = jnp.maximum(m, jnp.max(s, axis=1, keepdims=True))
        p = jnp.exp(s - m_new).astype(BF16)
        ones_v = jnp.where(lo, 1.0, kv).astype(BF16)
        acc = jnp.exp(m - m_new) * acc + jnp.dot(p, ones_v, preferred_element_type=F32)
        return m_new, acc

    def finish(carry):
        _, acc = carry
        return acc / jnp.where(lo, 1.0, jnp.maximum(pltpu.roll(acc, hd, axis=1), 1e-30))

    init = (jnp.full((rows, 1), NEG, F32), jnp.zeros((rows, LANES), F32))

    def sel_step(c, carry):
        k0 = pl.multiple_of(c * tk, tk)

        def penalty(dist, kpos):
            chosen = jnp.dot(sel, _block_expand(n_pair, kpos), preferred_element_type=F32)
            return jnp.where(dist >= 0, jnp.where(chosen > 0.5, 0.0, NEG), NEG)

        return attend(carry, k0, sel_ref[pl.ds(k0, tk), :], penalty)

    o_s = finish(lax.fori_loop(0, qi, sel_step, sel_step(qi, init)))

    w0 = jnp.maximum(q0 - C_WINDOW, 0)
    n_win = (tq + C_WINDOW) // tk
    own = (q0 - w0) // tk

    def win_step(i, carry):
        k0 = pl.multiple_of(w0 + lax.rem(own + i, n_win) * tk, tk)
        penalty = lambda dist, kpos: jnp.where(dist >= 0, jnp.where(dist < C_WINDOW, 0.0, NEG), NEG)
        return attend(carry, k0, win_ref[pl.ds(k0, tk), :], penalty)

    o_w = finish(lax.fori_loop(1, n_win, win_step, win_step(0, init)))

    o = gate(0) * o_c + gate(1) * o_s + gate(2) * o_w
    head = lambda j: o[j * tq:(j + 1) * tq]
    o_ref[...] = jnp.concatenate([jnp.where(lo, pltpu.roll(head(2 * i), hd, axis=1), head(2 * i + 1))
                                  for i in range(C_HPG // 2)], axis=1)


def _nsa_prompt_layer(x, batch, seq, prm, ln_g, ln_b):
    w_in, cmp_wk, cmp_wv, w_out = prm
    hd, grp = C_HEAD_DIM, C_KV_HEADS
    tq = tk = 256
    assert seq % tq == 0 and seq >= tq + C_WINDOW and C_WINDOW % tk == 0 and seq % (2 * C_CMP_BLOCK) == 0
    assert tq == tk and seq == 2 * C_CMP_BLOCK * (seq // C_SEL_BLOCK)
    kv_w = lambda i: w_in[:, C_WIDTH + C_KV_WIDTH * i:C_WIDTH + C_KV_WIDTH * (i + 1)].reshape(D_MODEL, grp, hd)
    kv_pair = lambda a, b: jnp.concatenate([kv_w(a), kv_w(b)], axis=2).reshape(D_MODEL, grp * 2 * hd)
    g_lo = C_WIDTH + 6 * C_KV_WIDTH
    g_w = w_in[:, g_lo:g_lo + 3 * C_HEADS].reshape(D_MODEL, grp, 3 * C_HPG)
    g_w = jnp.pad(g_w, ((0, 0), (0, 0), (0, LANES - 3 * C_HPG))).reshape(D_MODEL, grp * LANES)
    w_p = jnp.concatenate([w_in[:, :C_WIDTH], w_in[:, g_lo + 3 * C_HEADS:], kv_pair(0, 1), kv_pair(2, 3),
                           kv_pair(4, 5), g_w, w_in[:, C_WIDTH:C_WIDTH + 4 * C_KV_WIDTH]], axis=1)
    n_main = 2 * C_WIDTH + 4 * grp * 2 * hd
    p, *new_rows = _proj(x, w_p.astype(BF16), 256, (n_main,) + (C_KV_WIDTH,) * 4)
    kv_lo = 2 * C_WIDTH
    kvw = grp * 2 * hd
    cw = jnp.concatenate([jnp.broadcast_to(cmp_wk[:, None], (C_CMP_BLOCK, hd)),
                          jnp.broadcast_to(cmp_wv[:, None], (C_CMP_BLOCK, hd))], axis=1)
    ce, co = _cmp(p, kv_lo // kvw, jnp.tile(cw, (1, grp)), batch, seq)
    slopes = jnp.broadcast_to(jnp.pad(_alibi_slopes().reshape(grp, C_HPG), ((0, 0), (0, 8 - C_HPG)))[:, :, None],
                              (grp, 8, LANES))
    nq = seq // tq
    n_pair = seq // (2 * C_CMP_BLOCK)
    gw = 2 * hd
    o = pl.pallas_call(
        functools.partial(_nsa_prompt_kernel, tq=tq, tk=tk),
        grid=(batch, grp, nq),
        in_specs=[
            pl.BlockSpec((tq, C_HPG * hd), lambda b, g, i: (b * nq + i, g)),
            pl.BlockSpec((tq, LANES), lambda b, g, i: (b * nq + i, (kv_lo + 3 * kvw) // LANES + g)),
            pl.BlockSpec((1, n_pair, gw), lambda b, g, i: (b, 0, g)),
            pl.BlockSpec((1, n_pair, gw), lambda b, g, i: (b, 0, g)),
            pl.BlockSpec((seq, gw), lambda b, g, i: (b, (kv_lo + kvw) // gw + g)),
            pl.BlockSpec((seq, gw), lambda b, g, i: (b, (kv_lo + 2 * kvw) // gw + g)),
            pl.BlockSpec((1, 8, LANES), lambda b, g, i: (g, 0, 0)),
        ],
        out_specs=pl.BlockSpec((tq, C_HPG * hd), lambda b, g, i: (b * nq + i, g)),
        out_shape=jax.ShapeDtypeStruct((batch * seq, C_WIDTH), F32),
        compiler_params=_cparams(3),
        name="nsa_prompt",
    )(p, p, ce, co, p, p, slopes)
    x_new = _outproj_ln(o, p, 1, x, w_out.astype(BF16), ln_g, ln_b, 512)
    rows = tuple(r.reshape(batch, seq, grp, hd) for r in new_rows)
    keep = min(C_WINDOW, seq)
    win = p.reshape(batch, seq, -1)[:, seq - keep:, kv_lo + 2 * kvw:kv_lo + 3 * kvw].reshape(batch, keep, grp, 2, hd)
    return x_new, rows, win[:, :, :, 0], win[:, :, :, 1]


def _nsa_sample_kernel(tbl_ref, *refs, n_pages, ts):
    del tbl_ref
    ck, cv, sk, sv = (refs[i * n_pages:(i + 1) * n_pages] for i in range(4))
    (q_ref, g_ref, ksn_ref, vsn_ref, kwn_ref, vwn_ref, wk_ref, wv_ref, cwk_ref, cwv_ref, slope_ref,
     o_ref, ke_s, ko_s, ve_s, vo_s, new_s) = refs[4 * n_pages:]
    hd, grp = C_HEAD_DIM, C_KV_HEADS
    rows = C_HEADS * ts
    past = n_pages * PAGE_SIZE
    n_pair = past // C_SEL_BLOCK
    qpos = past + lax.rem(lax.broadcasted_iota(jnp.int32, (rows, 1), 0), ts)
    qb = (q_ref[0] * C_SCALE).astype(BF16)
    slope = slope_ref[:, 0:1]

    per_page = PAGE_SIZE // C_CMP_BLOCK
    for pages, cw_ref, e_s, o_s in ((ck, cwk_ref, ke_s, ko_s), (cv, cwv_ref, ve_s, vo_s)):
        cw = cw_ref[...][None]
        for pg in range(n_pages):
            tok = jnp.sum(pages[pg][0].reshape(per_page, C_CMP_BLOCK, grp * hd) * cw, axis=1)
            for i in range(per_page):
                n = pg * per_page + i
                dst = e_s if n % 2 == 0 else o_s
                dst[n // 2:n // 2 + 1, :] = tok[i:i + 1]

    pair = lax.broadcasted_iota(jnp.int32, (1, n_pair), 1)
    scores, masks = [], []
    for par, k_s in ((0, ke_s), (1, ko_s)):
        dist = qpos - ((2 * pair + par + 1) * C_CMP_BLOCK - 1)
        scores.append(_dot_t(qb, k_s[...].astype(BF16)) - slope * dist.astype(F32))
        masks.append(dist >= 0)
    p_e, p_o = _masked_softmax_parts(scores, masks)
    o_c = (jnp.dot(p_e.astype(BF16), ve_s[...].astype(BF16), preferred_element_type=F32)
           + jnp.dot(p_o.astype(BF16), vo_s[...].astype(BF16), preferred_element_type=F32))
    imp_h = p_e + p_o
    sel_rows = []
    for g in range(grp):
        base = g * C_HPG * ts
        imp = imp_h[base:base + ts]
        for j in range(1, C_HPG):
            imp = imp + imp_h[base + j * ts:base + (j + 1) * ts]
        sel_g = _top_n_mask(imp, C_TOP_N - 1)
        sel_rows += [sel_g] * C_HPG
    sel_rows = jnp.concatenate(sel_rows, axis=0).astype(BF16)

    lane_pos = lax.broadcasted_iota(jnp.int32, (1, PAGE_SIZE), 1)

    def new_rows(slot, k_ref, v_ref):
        new_s[2 * slot:2 * slot + 2] = jnp.zeros((2,) + new_s.shape[1:], F32)
        new_s[2 * slot, 0:ts, :] = k_ref[...]
        new_s[2 * slot + 1, 0:ts, :] = v_ref[...]
        return new_s[2 * slot].astype(BF16), new_s[2 * slot + 1].astype(BF16)

    def softmax_attend(tiles):
        scores = []
        for kb, _, kpos, pen in tiles:
            dist = qpos - kpos
            scores.append((_dot_t(qb, kb) - slope * dist.astype(F32)) + pen(dist))
        top = scores[0]
        for s in scores[1:]:
            top = jnp.maximum(top, s)
        m = jnp.max(top, axis=1, keepdims=True)
        probs = [jnp.exp(s - m) for s in scores]
        total = probs[0]
        for p in probs[1:]:
            total = total + p
        acc = None
        for p, (_, vb, _, _) in zip(probs, tiles):
            part = jnp.dot(p.astype(BF16), vb, preferred_element_type=F32)
            acc = part if acc is None else acc + part
        return acc / jnp.maximum(jnp.sum(total, axis=1, keepdims=True), 1e-30)

    causal = lambda dist: jnp.where(dist >= 0, 0.0, NEG)
    window = lambda dist: jnp.where(dist >= 0, jnp.where(dist < C_WINDOW, 0.0, NEG), NEG)

    tiles = []
    for pg in range(n_pages):
        kpos = pg * PAGE_SIZE + lane_pos
        chosen = jnp.dot(sel_rows, _block_expand(n_pair, kpos), preferred_element_type=F32)
        pen = lambda dist, chosen=chosen: jnp.where(dist >= 0, jnp.where(chosen > 0.5, 0.0, NEG), NEG)
        tiles.append((sk[pg][0].astype(BF16), sv[pg][0].astype(BF16), kpos, pen))
    tiles.append(new_rows(0, ksn_ref, vsn_ref) + (past + lane_pos, causal))
    o_s = softmax_attend(tiles)

    n_win = wk_ref.shape[1]
    tiles = []
    for c in range(n_win // PAGE_SIZE):
        lo = c * PAGE_SIZE
        tiles.append((wk_ref[0, lo:lo + PAGE_SIZE, :].astype(BF16), wv_ref[0, lo:lo + PAGE_SIZE, :].astype(BF16),
                      past - n_win + lo + lane_pos, window))
    tiles.append(new_rows(1, kwn_ref, vwn_ref) + (past + lane_pos, window))
    o_w = softmax_attend(tiles)

    per_group = C_HPG * ts
    diag = lambda full: jnp.concatenate(
        [full[g * per_group:(g + 1) * per_group, g * hd:(g + 1) * hd] for g in range(grp)], axis=0)
    gates = _sigmoid(g_ref[0])
    o_ref[0] = gates[:, 0:1] * diag(o_c) + gates[:, 1:2] * diag(o_s) + gates[:, 2:3] * diag(o_w)


def _nsa_sample_layer(x, batch, ts, caches, page_table, win_k, win_v, prm, ln_g, ln_b):
    w_in, cmp_wk, cmp_wv, w_out = prm
    hd, grp = C_HEAD_DIM, C_KV_HEADS
    n_pages = page_table.shape[1]
    n_win = win_k.shape[1]
    past = n_pages * PAGE_SIZE
    rows = C_HEADS * ts
    assert past % C_SEL_BLOCK == 0 and ts <= C_SEL_BLOCK and ts % 8 == 0 and n_win == C_WINDOW and past >= n_win
    assert ts & (ts - 1) == 0 and ts <= 16 and rows == LANES
    g_lo = C_WIDTH + 6 * C_KV_WIDTH
    w_s = jnp.concatenate([w_in[:, :C_WIDTH], w_in[:, g_lo + 3 * C_HEADS:], w_in[:, C_WIDTH:g_lo],
                           jnp.pad(w_in[:, g_lo:g_lo + 3 * C_HEADS], ((0, 0), (0, LANES - 3 * C_HEADS)))], axis=1)
    p = _proj(x, w_s.astype(BF16), 512)
    kv_lo = 2 * C_WIDTH
    q = p[:, :C_WIDTH].reshape(batch, ts, grp, C_HPG, hd).transpose(0, 2, 3, 1, 4).reshape(batch, grp, C_HPG * ts, hd)
    q_bd = (q[:, :, :, None, :] * jnp.eye(grp, dtype=F32)[None, :, None, :, None]).reshape(batch, rows, grp * hd)
    g_lo_p = kv_lo + 6 * C_KV_WIDTH
    g_t = p[:, g_lo_p:g_lo_p + 3 * C_HEADS].reshape(batch, ts, C_HEADS, 3).transpose(0, 2, 1, 3).reshape(batch, rows, 3)
    slope_rows = jnp.broadcast_to(jnp.repeat(_alibi_slopes(), ts)[:, None], (rows, LANES))
    bcast = lambda w: jnp.broadcast_to(w[:, None], (C_CMP_BLOCK, grp * hd))
    pools = [c.reshape(c.shape[0], PAGE_SIZE, grp * hd) for c in caches]
    page_spec = lambda pg: pl.BlockSpec((1, PAGE_SIZE, grp * hd), lambda b, tbl: (tbl[b, pg], 0, 0))
    new_spec = lambda i: pl.BlockSpec((ts, C_KV_WIDTH), lambda b, tbl: (b, kv_lo // C_KV_WIDTH + i))
    const2 = lambda shape: pl.BlockSpec(shape, lambda b, tbl: (0, 0))
    per_b = lambda shape: pl.BlockSpec((1,) + shape, lambda b, tbl: (b, 0, 0))
    in_specs = ([page_spec(pg) for _ in range(4) for pg in range(n_pages)]
                + [per_b((rows, grp * hd)), per_b((rows, 3)), new_spec(2), new_spec(3), new_spec(4), new_spec(5),
                   per_b((n_win, grp * hd)), per_b((n_win, grp * hd)),
                   const2((C_CMP_BLOCK, grp * hd)), const2((C_CMP_BLOCK, grp * hd)), const2((rows, LANES))])
    n_pair = past // C_SEL_BLOCK
    o = pl.pallas_call(
        functools.partial(_nsa_sample_kernel, n_pages=n_pages, ts=ts),
        grid_spec=pltpu.PrefetchScalarGridSpec(
            num_scalar_prefetch=1,
            grid=(batch,),
            in_specs=in_specs,
            out_specs=per_b((rows, hd)),
            scratch_shapes=[pltpu.VMEM((n_pair, grp * hd), F32)] * 4 + [pltpu.VMEM((4, PAGE_SIZE, grp * hd), F32)],
        ),
        out_shape=jax.ShapeDtypeStruct((batch, rows, hd), F32),
        compiler_params=_cparams(1),
        name="nsa_sample",
    )(page_table, *[pool for pool in pools for _ in range(n_pages)], q_bd, g_t, p, p, p, p,
      win_k.reshape(batch, n_win, grp * hd), win_v.reshape(batch, n_win, grp * hd), bcast(cmp_wk), bcast(cmp_wv),
      slope_rows)
    new = lambda i: p[:, kv_lo + i * C_KV_WIDTH:kv_lo + (i + 1) * C_KV_WIDTH].reshape(batch, ts, grp, hd)
    y = o.reshape(batch, C_HEADS, ts, hd).transpose(0, 2, 1, 3).reshape(batch * ts, C_WIDTH)
    x_new = _outproj_ln(y, p, 1, x, w_out.astype(BF16), ln_g, ln_b, 512)
    keep = lambda buf, i: jnp.concatenate([buf, new(i)], axis=1)[:, -n_win:]
    return x_new, (new(0), new(1), new(2), new(3)), keep(win_k, 4), keep(win_v, 5)


def kernel(x_prompt, x_sample, state_rwkv_S, state_rwkv_shift, state_pool, cache_cmp_k, cache_cmp_v, cache_sel_k,
           cache_sel_v, state_win_k, state_win_v, page_table, ln_g, ln_b, a_w_in, a_mu, a_w0, a_w2, a_a0, a_a2, a_k_k,
           a_k_a, a_r_k, a_lnx_g, a_lnx_b, a_w_out, b_w_in, b_w_grp, b_scale, b_w_out, c_w_in, c_cmp_wk, c_cmp_wv,
           c_w_out):
    bp, tp, _ = x_prompt.shape
    bs, ts, _ = x_sample.shape
    xp = x_prompt.reshape(bp * tp, D_MODEL)
    xs = x_sample.reshape(bs * ts, D_MODEL)
    s_p, s_s, sh_p, sh_s, pl_p, pl_s = [], [], [], [], [], []
    rows_p, rows_s, wk_p, wk_s, wv_p, wv_s = [], [], [], [], [], []
    for layer in range(DEPTH):
        kind, li = layer % 3, layer // 3
        g, b = ln_g[layer], ln_b[layer]
        if kind == 0:
            prm = (a_w_in[li], a_mu[li], a_w0[li], a_w2[li], a_a0[li], a_a2[li], a_k_k[li], a_k_a[li], a_r_k[li],
                   a_lnx_g[li], a_lnx_b[li], a_w_out[li])
            xp, s_new, sh_new = _rwkv_layer(xp, bp, tp, jnp.zeros((bp, A_HEADS, A_HEAD_DIM, A_HEAD_DIM), F32),
                                            jnp.zeros((bp, A_NCOLS), F32), prm, g, b, 512, 256)
            s_p.append(s_new)
            sh_p.append(sh_new)
            xs, s_new, sh_new = _rwkv_layer(xs, bs, ts, state_rwkv_S[li], state_rwkv_shift[li], prm, g, b, 512, 64)
            s_s.append(s_new)
            sh_s.append(sh_new)
        elif kind == 1:
            prm = (b_w_in[li], b_w_grp[li], b_scale[li], b_w_out[li])
            xp, buf_new = _pool_layer(xp, bp, tp, jnp.zeros((bp, B_BUF, B_WIDTH), F32), False, prm, g, b, 512, 512)
            pl_p.append(buf_new)
            xs, buf_new = _pool_layer(xs, bs, ts, state_pool[li], True, prm, g, b, 512, ts)
            pl_s.append(buf_new)
        else:
            prm = (c_w_in[li], c_cmp_wk[li], c_cmp_wv[li], c_w_out[li])
            xp, rows, wk, wv = _nsa_prompt_layer(xp, bp, tp, prm, g, b)
            rows_p.append(rows)
            wk_p.append(wk)
            wv_p.append(wv)
            caches = (cache_cmp_k[li], cache_cmp_v[li], cache_sel_k[li], cache_sel_v[li])
            xs, rows, wk, wv = _nsa_sample_layer(xs, bs, ts, caches, page_table, state_win_k[li], state_win_v[li],
                                                 prm, g, b)
            rows_s.append(rows)
            wk_s.append(wk)
            wv_s.append(wv)
    stack = jnp.stack
    return (xp.reshape(bp, tp, D_MODEL), xs.reshape(bs, ts, D_MODEL), stack(s_p), stack(s_s), stack(sh_p), stack(sh_s),
            stack(pl_p), stack(pl_s),
            stack([r[0] for r in rows_p]), stack([r[0] for r in rows_s]),
            stack([r[1] for r in rows_p]), stack([r[1] for r in rows_s]),
            stack([r[2] for r in rows_p]), stack([r[2] for r in rows_s]),
            stack([r[3] for r in rows_p]), stack([r[3] for r in rows_s]),
            stack(wk_p), stack(wk_s), stack(wv_p), stack(wv_s))
```
